```python
import math, functools
import jax, jax.numpy as jnp
from jax import lax
import numpy as np

D_MODEL = 2048
BATCH = 8
SEQ = 2048
DEPTH = 1
DEC_BATCH = 32
DEC_SEQ = 8
PAST_LEN = 8192
PAGE_SIZE = 128

N_HEADS = 16
HEAD_DIM = 128
ATTN_WIDTH = N_HEADS * HEAD_DIM
MOBA_BLOCK = 256
MOBA_TOPK = 3
Q_BLOCK = 32
NUM_BUCKETS = 32
MAX_DISTANCE = 128
CONV_CH = D_MODEL
CONV_WIDTH = 31
PEER_HEADS = 8
PEER_NKEYS = 128
PEER_EXPERTS = PEER_NKEYS * PEER_NKEYS
PEER_QDIM = 256
PEER_TOPK = 16
PEER_TOK_BLOCK = 128
IN_COLS = 2 * CONV_CH + 3 * ATTN_WIDTH + 2 * D_MODEL
EPS = 1e-6
NEG = -1e30

kernel_name = "hybrid_conformer_moba_peer_step"


def rmsnorm(x, g):
    xf = x.astype(jnp.float32)
    y = xf * lax.rsqrt(jnp.mean(xf * xf, axis=-1, keepdims=True) + EPS) * g.astype(jnp.float32)
    return y.astype(x.dtype)


def layernorm(x, g, b):
    xf = x.astype(jnp.float32)
    mu = jnp.mean(xf, axis=-1, keepdims=True)
    xc = xf - mu
    y = xc * lax.rsqrt(jnp.mean(xc * xc, axis=-1, keepdims=True) + EPS) * g.astype(jnp.float32) + b.astype(jnp.float32)
    return y.astype(x.dtype)


def rel_bucket(dist):
    n = jnp.maximum(dist, 0)
    max_exact = NUM_BUCKETS // 2
    large = max_exact + (jnp.log(jnp.maximum(n, 1).astype(jnp.float32) / max_exact)
                         / math.log(MAX_DISTANCE / max_exact) * (NUM_BUCKETS - max_exact)).astype(jnp.int32)
    large = jnp.minimum(large, NUM_BUCKETS - 1)
    return jnp.where(n < max_exact, n, large)


def moba_chunk(q, q_pos, kb, vb, means, rel_bias):
    n_blocks = kb.shape[1]
    k_sel = min(MOBA_TOPK, n_blocks)
    qb = q.shape[0]
    cur = q_pos // MOBA_BLOCK
    scores = jnp.einsum("qhd,hnd->hqn", q.astype(jnp.float32), means)
    past = jnp.arange(n_blocks)[None, None, :] < cur[None, :, None]
    scores = jnp.where(past, scores, NEG)
    _, sel = lax.top_k(scores, k_sel)
    own = jnp.broadcast_to(cur[None, :, None], (N_HEADS, qb, 1)).astype(sel.dtype)
    idx = jnp.concatenate([sel, own], axis=-1)
    sel_ok = jnp.broadcast_to(jnp.arange(k_sel)[None, None, :] < cur[None, :, None], (N_HEADS, qb, k_sel))
    blk_ok = jnp.concatenate([sel_ok, jnp.ones((N_HEADS, qb, 1), bool)], axis=-1)
    heads = jnp.arange(N_HEADS)[:, None, None]
    k_g = kb[heads, idx]
    v_g = vb[heads, idx]
    k_pos = idx[..., None] * MOBA_BLOCK + jnp.arange(MOBA_BLOCK)
    dist = q_pos[None, :, None, None] - k_pos
    mask = blk_ok[..., None] & (dist >= 0)
    bias = rel_bias[rel_bucket(dist), heads[..., None]].astype(jnp.float32)
    logits = jnp.einsum("qhd,hqsbd->hqsb", q, k_g, preferred_element_type=jnp.float32) * (HEAD_DIM ** -0.5) + bias
    logits = jnp.where(mask, logits, NEG)
    p = jax.nn.softmax(logits.reshape(N_HEADS, qb, -1), axis=-1).reshape(logits.shape)
    return jnp.einsum("hqsb,hqsbd->qhd", p.astype(v_g.dtype), v_g)


def moba_sequence(q, q_start, k, v, rel_bias):
    n_q, n_k = q.shape[0], k.shape[0]
    n_blocks = -(-n_k // MOBA_BLOCK)
    pad = n_blocks * MOBA_BLOCK - n_k
    kb = jnp.pad(k, ((0, pad), (0, 0), (0, 0))).reshape(n_blocks, MOBA_BLOCK, N_HEADS, HEAD_DIM).transpose(2, 0, 1, 3)
    vb = jnp.pad(v, ((0, pad), (0, 0), (0, 0))).reshape(n_blocks, MOBA_BLOCK, N_HEADS, HEAD_DIM).transpose(2, 0, 1, 3)
    means = jnp.mean(kb.astype(jnp.float32), axis=2)
    qb = math.gcd(n_q, Q_BLOCK)
    n_chunks = n_q // qb
    q_c = q.reshape(n_chunks, qb, N_HEADS, HEAD_DIM)
    pos = (q_start + jnp.arange(n_q, dtype=jnp.int32)).reshape(n_chunks, qb)
    out = lax.map(lambda a: moba_chunk(a[0], a[1], kb, vb, means, rel_bias), (q_c, pos))
    return out.reshape(n_q, N_HEADS, HEAD_DIM)


def moba_prompt(q, k, v, rel_bias):
    return lax.map(lambda a: moba_sequence(a[0], 0, a[1], a[2], rel_bias), (q, k, v))


def moba_sample(q, k, v, cache_k_l, cache_v_l, page_table, rel_bias):
    past_len = page_table.shape[1] * cache_k_l.shape[1]

    def one(a):
        q_n, k_n, v_n, pages = a
        k_all = jnp.concatenate([cache_k_l[pages].reshape(past_len, N_HEADS, HEAD_DIM), k_n], axis=0)
        v_all = jnp.concatenate([cache_v_l[pages].reshape(past_len, N_HEADS, HEAD_DIM), v_n], axis=0)
        return moba_sequence(q_n, past_len, k_all, v_all, rel_bias)

    return lax.map(one, (q, k, v, page_table))


def conformer_conv(conv_in, prefix, w_dw, b_dw, ln_g, ln_b, w_pw):
    a, gate = jnp.split(conv_in, 2, axis=-1)
    u = a * jax.nn.sigmoid(gate)
    u_ext = jnp.concatenate([prefix.astype(u.dtype), u], axis=1)
    y = lax.conv_general_dilated(u_ext, w_dw[:, None, :].astype(u.dtype), window_strides=(1,), padding="VALID",
                                 dimension_numbers=("NWC", "WIO", "NWC"), feature_group_count=CONV_CH) + b_dw
    y = jax.nn.silu(layernorm(y, ln_g, ln_b))
    return y @ w_pw, u_ext[:, -(CONV_WIDTH - 1):]


def peer_block(xc, wq, subkeys, u_tab, v_tab):
    tb = xc.shape[0]
    q = (xc @ wq).reshape(tb, PEER_HEADS, 2, PEER_QDIM // 2)
    s = jnp.einsum("thpd,hpnd->thpn", q, subkeys, preferred_element_type=jnp.float32)
    s1, i1 = lax.top_k(s[:, :, 0], PEER_TOPK)
    s2, i2 = lax.top_k(s[:, :, 1], PEER_TOPK)
    cand = (s1[..., :, None] + s2[..., None, :]).reshape(tb, PEER_HEADS, PEER_TOPK * PEER_TOPK)
    top, ci = lax.top_k(cand, PEER_TOPK)
    expert = (jnp.take_along_axis(i1, ci // PEER_TOPK, axis=-1) * PEER_NKEYS
              + jnp.take_along_axis(i2, ci % PEER_TOPK, axis=-1))
    g = jax.nn.softmax(top, axis=-1)
    act = jax.nn.gelu(jnp.einsum("td,thkd->thk", xc, u_tab[expert], preferred_element_type=jnp.float32))
    return jnp.einsum("thk,thkd->td", (g * act).astype(xc.dtype), v_tab[expert])


def peer_ffn(x, wq, subkeys, u_tab, v_tab):
    bsz, n_t, d = x.shape
    n = bsz * n_t
    tb = min(PEER_TOK_BLOCK, n)
    n_blk = -(-n // tb)
    xt = jnp.pad(x.reshape(n, d), ((0, n_blk * tb - n), (0, 0))).reshape(n_blk, tb, d)
    out = lax.map(lambda xc: peer_block(xc, wq, subkeys, u_tab, v_tab), xt)
    return out.reshape(n_blk * tb, d)[:n].reshape(bsz, n_t, d)


def trunk_layer(h, conv_prefix, attend, norm_mix_g, w_in, b_gate, conv_w_dw, conv_b_dw, conv_ln_g, conv_ln_b,
                w_conv_out, w_attn_out, w_out, norm_ffn_g, peer_wq, peer_subkeys, peer_u, peer_v):
    bsz, n_t, _ = h.shape
    xn = rmsnorm(h, norm_mix_g)
    proj = xn @ w_in
    c1 = 2 * CONV_CH
    c2 = c1 + ATTN_WIDTH
    c3 = c2 + ATTN_WIDTH
    c4 = c3 + ATTN_WIDTH
    hs = (bsz, n_t, N_HEADS, HEAD_DIM)
    q = proj[..., c1:c2].reshape(hs)
    k = proj[..., c2:c3].reshape(hs)
    v = proj[..., c3:c4].reshape(hs)
    conv_out, conv_state = conformer_conv(proj[..., :c1], conv_prefix, conv_w_dw, conv_b_dw, conv_ln_g, conv_ln_b, w_conv_out)
    attn_out = attend(q, k, v).reshape(bsz, n_t, ATTN_WIDTH) @ w_attn_out
    gates = jax.nn.sigmoid((proj[..., c4:] + b_gate).astype(jnp.float32)).astype(h.dtype)
    merged = gates[..., :D_MODEL] * conv_out + gates[..., D_MODEL:] * attn_out
    h = h + merged @ w_out
    h = h + peer_ffn(rmsnorm(h, norm_ffn_g), peer_wq, peer_subkeys, peer_u, peer_v)
    return h, k, v, conv_state


def setup_inputs(seed: int = 0) -> dict:
    key = jax.random.key(seed)
    ks = jax.random.split(key, 24)
    f32 = jnp.float32
    n_pages = PAST_LEN // PAGE_SIZE
    n_used = DEC_BATCH * n_pages
    n_pool = n_used + n_used // 4

    def nrm(k, shape, scale):
        return jax.random.normal(k, shape, f32) * scale

    page_table = jax.random.permutation(ks[5], n_pool)[:n_used].reshape(DEC_BATCH, n_pages).astype(jnp.int32)
    return {
        "x_prompt": nrm(ks[0], (BATCH, SEQ, D_MODEL), 1.0),
        "x_sample": nrm(ks[1], (DEC_BATCH, DEC_SEQ, D_MODEL), 1.0),
        "cache_k": nrm(ks[2], (DEPTH, n_pool, PAGE_SIZE, N_HEADS, HEAD_DIM), 1.0),
        "cache_v": nrm(ks[3], (DEPTH, n_pool, PAGE_SIZE, N_HEADS, HEAD_DIM), 1.0),
        "state_conv": nrm(ks[4], (DEPTH, DEC_BATCH, CONV_WIDTH - 1, CONV_CH), 0.5),
        "page_table": page_table,
        "rel_bias": nrm(ks[6], (NUM_BUCKETS, N_HEADS), 0.5),
        "norm_mix_g": 1.0 + nrm(ks[7], (DEPTH, D_MODEL), 0.02),
        "w_in": nrm(ks[8], (DEPTH, D_MODEL, IN_COLS), D_MODEL ** -0.5),
        "b_gate": nrm(ks[9], (DEPTH, 2 * D_MODEL), 0.02),
        "conv_w_dw": nrm(ks[10], (DEPTH, CONV_WIDTH, CONV_CH), CONV_WIDTH ** -0.5),
        "conv_b_dw": nrm(ks[11], (DEPTH, CONV_CH), 0.02),
        "conv_ln_g": 1.0 + nrm(ks[12], (DEPTH, CONV_CH), 0.02),
        "conv_ln_b": nrm(ks[13], (DEPTH, CONV_CH), 0.02),
        "w_conv_out": nrm(ks[14], (DEPTH, CONV_CH, D_MODEL), CONV_CH ** -0.5),
        "w_attn_out": nrm(ks[15], (DEPTH, ATTN_WIDTH, D_MODEL), ATTN_WIDTH ** -0.5),
        "w_out": nrm(ks[16], (DEPTH, D_MODEL, D_MODEL), D_MODEL ** -0.5),
        "norm_ffn_g": 1.0 + nrm(ks[17], (DEPTH, D_MODEL), 0.02),
        "peer_wq": nrm(ks[18], (DEPTH, D_MODEL, PEER_HEADS * PEER_QDIM), D_MODEL ** -0.5),
        "peer_subkeys": nrm(ks[19], (DEPTH, PEER_HEADS, 2, PEER_NKEYS, PEER_QDIM // 2), (PEER_QDIM // 2) ** -0.5),
        "peer_u": nrm(ks[20], (DEPTH, PEER_EXPERTS, D_MODEL), D_MODEL ** -0.5),
        "peer_v": nrm(ks[21], (DEPTH, PEER_EXPERTS, D_MODEL), 0.1),
        "norm_final_g": 1.0 + nrm(ks[22], (D_MODEL,), 0.02),
    }


def reference(x_prompt, x_sample, cache_k, cache_v, state_conv, page_table, rel_bias, norm_mix_g, w_in, b_gate,
              conv_w_dw, conv_b_dw, conv_ln_g, conv_ln_b, w_conv_out, w_attn_out, w_out, norm_ffn_g,
              peer_wq, peer_subkeys, peer_u, peer_v, norm_final_g):
    hp, hs = x_prompt, x_sample
    kp_l, vp_l, ks_l, vs_l, cp_l, cs_l = [], [], [], [], [], []
    attend_p = functools.partial(moba_prompt, rel_bias=rel_bias)
    for l in range(DEPTH):
        lw = (norm_mix_g[l], w_in[l], b_gate[l], conv_w_dw[l], conv_b_dw[l], conv_ln_g[l], conv_ln_b[l],
              w_conv_out[l], w_attn_out[l], w_out[l], norm_ffn_g[l], peer_wq[l], peer_subkeys[l], peer_u[l], peer_v[l])
        prefix_p = jnp.zeros((hp.shape[0], CONV_WIDTH - 1, CONV_CH), hp.dtype)
        hp, kp, vp, cp = trunk_layer(hp, prefix_p, attend_p, *lw)
        attend_s = functools.partial(moba_sample, cache_k_l=cache_k[l], cache_v_l=cache_v[l],
                                     page_table=page_table, rel_bias=rel_bias)
        hs, k_s, v_s, c_s = trunk_layer(hs, state_conv[l], attend_s, *lw)
        kp_l.append(kp)
        vp_l.append(vp)
        ks_l.append(k_s)
        vs_l.append(v_s)
        cp_l.append(cp)
        cs_l.append(c_s)
    y_prompt = rmsnorm(hp, norm_final_g)
    y_sample = rmsnorm(hs, norm_final_g)
    return (y_prompt, y_sample, jnp.stack(kp_l), jnp.stack(vp_l), jnp.stack(ks_l), jnp.stack(vs_l),
            jnp.stack(cp_l), jnp.stack(cs_l))
```

```python
import functools
import math

import jax
import jax.numpy as jnp
from jax import lax
from jax.experimental import pallas as pl
from jax.experimental.pallas import tpu as pltpu

f32 = jnp.float32
bf16 = jnp.bfloat16

MOBA_BLOCK = 256
MOBA_TOPK = 3
NUM_BUCKETS = 32
MAX_DISTANCE = 128
PEER_TOPK = 16
EPS = 1e-6
NEG = -1e30

V7X_VMEM_BYTES = 64 * 1024 * 1024
V7X_LANES = 128
V7X_SUBLANES = 8
VMEM_CAP_BYTES = V7X_VMEM_BYTES - 8 * 1024 * 1024

NT_DIMS = (((1,), (1,)), ((), ()))
TN_DIMS = (((0,), (0,)), ((), ()))


def _params(semantics, est_bytes):
    limit = int(min(max(est_bytes, 16 * 1024 * 1024), VMEM_CAP_BYTES))
    return pltpu.CompilerParams(dimension_semantics=semantics, vmem_limit_bytes=limit)


def _nbytes(shape, dtype):
    return math.prod(shape) * jnp.dtype(dtype).itemsize


def _rmsnorm_kernel(x_ref, g_ref, o_ref):
    x = x_ref[...]
    ms = jnp.mean(x * x, axis=-1, keepdims=True)
    o_ref[...] = (x * lax.rsqrt(ms + EPS) * g_ref[...]).astype(o_ref.dtype)


def rmsnorm(x, g, out_dtype=bf16):
    t, d = x.shape
    tr = min(512, t)
    est = 2 * (_nbytes((tr, d), f32) + _nbytes((tr, d), out_dtype)) + 4 * _nbytes((tr, d), f32)
    return pl.pallas_call(
        _rmsnorm_kernel,
        out_shape=jax.ShapeDtypeStruct((t, d), out_dtype),
        grid=(t // tr,),
        in_specs=[pl.BlockSpec((tr, d), lambda i: (i, 0)), pl.BlockSpec((1, d), lambda i: (0, 0))],
        out_specs=pl.BlockSpec((tr, d), lambda i: (i, 0)),
        compiler_params=_params(("parallel",), est),
        name="rmsnorm",
    )(x, g.reshape(1, d))


def _mm_kernel(x_ref, w_ref, o_ref):
    o_ref[...] = jnp.dot(x_ref[...], w_ref[...], preferred_element_type=f32).astype(o_ref.dtype)


def _mm_res_kernel(x_ref, w_ref, r_ref, o_ref):
    acc = jnp.dot(x_ref[...], w_ref[...], preferred_element_type=f32)
    o_ref[...] = (r_ref[...] + acc).astype(o_ref.dtype)


def matmul(x, w, *, out_dtype=f32, residual=None, tm=512, tn=1024):
    m, k = x.shape
    n = w.shape[1]
    tm, tn = min(tm, m), min(tn, n)
    est = 2 * (_nbytes((tm, k), bf16) + _nbytes((k, tn), bf16) + _nbytes((tm, tn), out_dtype)
               + _nbytes((tm, tn), f32)) + _nbytes((tm, tn), f32)
    in_specs = [pl.BlockSpec((tm, k), lambda i, j: (i, 0)), pl.BlockSpec((k, tn), lambda i, j: (0, j))]
    args = [x, w]
    body = _mm_kernel
    if residual is not None:
        in_specs.append(pl.BlockSpec((tm, tn), lambda i, j: (i, j)))
        args.append(residual)
        body = _mm_res_kernel
    return pl.pallas_call(
        body,
        out_shape=jax.ShapeDtypeStruct((m, n), out_dtype),
        grid=(m // tm, n // tn),
        in_specs=in_specs,
        out_specs=pl.BlockSpec((tm, tn), lambda i, j: (i, j)),
        compiler_params=_params(("parallel", "parallel"), est),
        name="matmul",
    )(*args)


def _merge_kernel(hc_ref, at_ref, wc_ref, wa_ref, ga_ref, gb_ref, ba_ref, bb_ref, o_ref):
    conv_out = jnp.dot(hc_ref[...], wc_ref[...], preferred_element_type=f32)
    attn_out = jnp.dot(at_ref[...], wa_ref[...], preferred_element_type=f32)
    gate_a = jax.nn.sigmoid(ga_ref[...] + ba_ref[...])
    gate_b = jax.nn.sigmoid(gb_ref[...] + bb_ref[...])
    o_ref[...] = (gate_a * conv_out + gate_b * attn_out).astype(o_ref.dtype)


def gated_merge(hc, attn, w_conv_out, w_attn_out, proj, b_gate, gate_col0, *, tm=512, tn=512):
    m, k = hc.shape
    n = w_conv_out.shape[1]
    tm, tn = min(tm, m), min(tn, n)
    ga0, gb0 = gate_col0 // tn, (gate_col0 + n) // tn
    nb = n // tn
    est = 2 * (2 * _nbytes((tm, k), bf16) + 2 * _nbytes((k, tn), bf16) + 3 * _nbytes((tm, tn), f32)) \
        + 4 * _nbytes((tm, tn), f32)
    return pl.pallas_call(
        _merge_kernel,
        out_shape=jax.ShapeDtypeStruct((m, n), bf16),
        grid=(m // tm, nb),
        in_specs=[
            pl.BlockSpec((tm, k), lambda i, j: (i, 0)),
            pl.BlockSpec((tm, k), lambda i, j: (i, 0)),
            pl.BlockSpec((k, tn), lambda i, j: (0, j)),
            pl.BlockSpec((k, tn), lambda i, j: (0, j)),
            pl.BlockSpec((tm, tn), lambda i, j: (i, ga0 + j)),
            pl.BlockSpec((tm, tn), lambda i, j: (i, gb0 + j)),
            pl.BlockSpec((1, tn), lambda i, j: (0, j)),
            pl.BlockSpec((1, tn), lambda i, j: (0, nb + j)),
        ],
        out_specs=pl.BlockSpec((tm, tn), lambda i, j: (i, j)),
        compiler_params=_params(("parallel", "parallel"), est),
        name="gated_merge",
    )(hc, attn, w_conv_out, w_attn_out, proj, proj, b_gate, b_gate)


CONV_ROW_CHUNK = 32
CONV_LANE_CHUNK = 256
CONV_HALO = 32


def _conv_kernel(a_ref, gt_ref, pre_ref, wdw_ref, bdw_ref, lng_ref, lnb_ref, hc_ref, st_ref, ubuf, ybuf,
                 *, tt, width):
    t = pl.program_id(1)
    c = ubuf.shape[1]
    lead = CONV_HALO - (width - 1)
    rc = min(CONV_ROW_CHUNK, tt)

    @pl.when(t == 0)
    def _():
        ubuf[0:lead, :] = jnp.zeros((lead, c), f32)
        ubuf[lead:CONV_HALO, :] = pre_ref[...]

    ubuf[CONV_HALO:CONV_HALO + tt, :] = a_ref[...] * jax.nn.sigmoid(gt_ref[...])

    n_lane = c // CONV_LANE_CHUNK

    def chunk(idx, carry):
        r0 = pl.multiple_of((idx // n_lane) * rc, rc)
        c0 = pl.multiple_of((idx % n_lane) * CONV_LANE_CHUNK, CONV_LANE_CHUNK)
        win = ubuf[pl.ds(r0, rc + CONV_HALO), pl.ds(c0, CONV_LANE_CHUNK)]
        acc = jnp.broadcast_to(bdw_ref[:, pl.ds(c0, CONV_LANE_CHUNK)], (rc, CONV_LANE_CHUNK))
        for phase in range(V7X_SUBLANES):
            offs = [o for o in range(lead, lead + width) if o % V7X_SUBLANES == phase]
            if not offs:
                continue
            span = offs[-1] - phase + rc
            shifted = win[phase:phase + span]
            for o in offs:
                q = o - phase
                w_row = wdw_ref[o - lead:o - lead + 1, pl.ds(c0, CONV_LANE_CHUNK)]
                acc = acc + shifted[q:q + rc] * w_row
        ybuf[pl.ds(r0, rc), pl.ds(c0, CONV_LANE_CHUNK)] = acc
        return carry

    lax.fori_loop(0, (tt // rc) * n_lane, chunk, 0)

    def ln_chunk(i, carry):
        r0 = pl.multiple_of(i * rc, rc)
        y = ybuf[pl.ds(r0, rc), :]
        mu = jnp.mean(y, axis=-1, keepdims=True)
        yc = y - mu
        var = jnp.mean(yc * yc, axis=-1, keepdims=True)
        yn = yc * lax.rsqrt(var + EPS) * lng_ref[...] + lnb_ref[...]
        hc_ref[pl.ds(r0, rc), :] = (yn * jax.nn.sigmoid(yn)).astype(hc_ref.dtype)
        return carry

    lax.fori_loop(0, tt // rc, ln_chunk, 0)

    @pl.when(t == pl.num_programs(1) - 1)
    def _():
        st_ref[...] = ubuf[tt + lead:tt + CONV_HALO, :]

    ubuf[0:CONV_HALO, :] = ubuf[tt:tt + CONV_HALO, :]


def conformer_conv(proj, prefix, w_dw, b_dw, ln_g, ln_b, *, n_seq, seq_len, tt, out_dtype):
    width, c = w_dw.shape
    nt = seq_len // tt
    est = 2 * (2 * _nbytes((tt, c), f32) + _nbytes((width - 1, c), f32) + _nbytes((width, c), f32)
               + _nbytes((tt, c), out_dtype) + _nbytes((width - 1, c), f32)) \
        + _nbytes((2 * tt + CONV_HALO, c), f32) + 8 * _nbytes((CONV_ROW_CHUNK, c), f32)
    kern = functools.partial(_conv_kernel, tt=tt, width=width)
    return pl.pallas_call(
        kern,
        out_shape=(jax.ShapeDtypeStruct((n_seq * seq_len, c), out_dtype),
                   jax.ShapeDtypeStruct((n_seq, width - 1, c), f32)),
        grid=(n_seq, nt),
        in_specs=[
            pl.BlockSpec((tt, c), lambda b, t: (b * nt + t, 0)),
            pl.BlockSpec((tt, c), lambda b, t: (b * nt + t, 1)),
            pl.BlockSpec((None, width - 1, c), lambda b, t: (b, 0, 0)),
            pl.BlockSpec((width, c), lambda b, t: (0, 0)),
            pl.BlockSpec((1, c), lambda b, t: (0, 0)),
            pl.BlockSpec((1, c), lambda b, t: (0, 0)),
            pl.BlockSpec((1, c), lambda b, t: (0, 0)),
        ],
        out_specs=(pl.BlockSpec((tt, c), lambda b, t: (b * nt + t, 0)),
                   pl.BlockSpec((None, width - 1, c), lambda b, t: (b, 0, 0))),
        scratch_shapes=[pltpu.VMEM((tt + CONV_HALO, c), f32), pltpu.VMEM((tt, c), f32)],
        compiler_params=_params(("parallel", "arbitrary"), est),
        name="conformer_conv",
    )(proj, proj, prefix, w_dw, b_dw.reshape(1, c), ln_g.reshape(1, c), ln_b.reshape(1, c))


def _rel_bucket(dist):
    n = jnp.maximum(dist, 0)
    max_exact = NUM_BUCKETS // 2
    large = max_exact + (jnp.log(jnp.maximum(n, 1).astype(f32) / max_exact)
                         / math.log(MAX_DISTANCE / max_exact) * (NUM_BUCKETS - max_exact)).astype(jnp.int32)
    large = jnp.minimum(large, NUM_BUCKETS - 1)
    return jnp.where(n < max_exact, n, large)


def _bias_from_dist(dist, lookup):
    bucket = _rel_bucket(dist)
    out = jnp.zeros(dist.shape, f32)
    for b in range(NUM_BUCKETS):
        out = jnp.where(bucket == b, lookup(b), out)
    return jnp.where(dist >= 0, out, NEG)


def _prompt_bias_kernel(rb_ref, o_ref):
    h = pl.program_id(0)
    ki = lax.broadcasted_iota(jnp.int32, (MOBA_BLOCK, MOBA_BLOCK), 0)
    qi = lax.broadcasted_iota(jnp.int32, (MOBA_BLOCK, MOBA_BLOCK), 1)
    for tab in range(3):
        dist = tab * MOBA_BLOCK + qi - ki
        o_ref[tab] = _bias_from_dist(dist, lambda b: rb_ref[b, h])


def prompt_bias_tables(rel_bias):
    n_heads = rel_bias.shape[1]
    return pl.pallas_call(
        _prompt_bias_kernel,
        out_shape=jax.ShapeDtypeStruct((n_heads, 3, MOBA_BLOCK, MOBA_BLOCK), f32),
        grid=(n_heads,),
        in_specs=[pl.BlockSpec(memory_space=pltpu.SMEM)],
        out_specs=pl.BlockSpec((None, 3, MOBA_BLOCK, MOBA_BLOCK), lambda h: (h, 0, 0, 0)),
        compiler_params=_params(("parallel",), 16 * 1024 * 1024),
        name="prompt_bias_tables",
    )(rel_bias)


def _sample_bias_kernel(rb_ref, o_ref, *, past_len, page, n_q, n_near):
    n_cols = o_ref.shape[2]
    r = lax.broadcasted_iota(jnp.int32, (page, n_cols), 0)
    col = lax.broadcasted_iota(jnp.int32, (page, n_cols), 1)
    qi = col % n_q
    head = col // n_q

    def lookup(b):
        out = jnp.zeros((page, n_cols), f32)
        for hh in range(n_cols // n_q):
            out = jnp.where(head == hh, rb_ref[b, hh], out)
        return out

    n_pages = past_len // page
    o_ref[0] = _bias_from_dist(jnp.full((page, n_cols), MAX_DISTANCE, jnp.int32), lookup)
    for i in range(n_near):
        pg = n_pages - n_near + i
        o_ref[1 + i] = _bias_from_dist(past_len + qi - (pg * page + r), lookup)
    own = _bias_from_dist(qi - r, lookup)
    o_ref[1 + n_near] = jnp.where(r < n_q, own, NEG)


def sample_bias_tables(rel_bias, *, past_len, page, n_q, n_near):
    n_heads = rel_bias.shape[1]
    kern = functools.partial(_sample_bias_kernel, past_len=past_len, page=page, n_q=n_q, n_near=n_near)
    return pl.pallas_call(
        kern,
        out_shape=jax.ShapeDtypeStruct((n_near + 2, page, n_heads * n_q), f32),
        in_specs=[pl.BlockSpec(memory_space=pltpu.SMEM)],
        out_specs=pl.BlockSpec(memory_space=pltpu.VMEM),
        name="sample_bias_tables",
    )(rel_bias)


MOBA_HEAD_GROUP = 4


def _moba_prompt_kernel(q_ref, k_ref, v_ref, bias_ref, o_ref,
                        kb_ref, vt_ref, mean_ref, sel_ref, m_ref, l_ref, acc_ref, *, head_dim):
    c = pl.program_id(2)
    n_blocks = k_ref.shape[0] // MOBA_BLOCK
    scale = head_dim ** -0.5

    for hh in range(MOBA_HEAD_GROUP):
        lanes = slice(hh * head_dim, (hh + 1) * head_dim)

        @pl.when(c == 0)
        def _():
            kf = k_ref[:, lanes]
            kb_ref[hh] = kf.astype(bf16)
            vt_ref[hh] = v_ref[:, lanes].T.astype(bf16)
            mean_ref[hh] = jnp.mean(kf.reshape(n_blocks, MOBA_BLOCK, head_dim), axis=1)

        qf = q_ref[:, lanes]
        st = lax.dot_general(mean_ref[hh], qf, NT_DIMS, preferred_element_type=f32,
                             precision=lax.Precision.HIGHEST)
        row = lax.broadcasted_iota(jnp.int32, st.shape, 0)
        sm = jnp.where(row < c, st, NEG)
        for j in range(n_blocks):
            rj = sm[j:j + 1, :]
            beats = jnp.where((sm > rj) | ((sm == rj) & (row < j)), 1.0, 0.0)
            rank = jnp.sum(beats, axis=0, keepdims=True)
            jv = jnp.full(rank.shape, j, jnp.int32)
            chosen = jnp.where((rank < MOBA_TOPK) & (jv < c), 1.0, 0.0)
            sel_ref[j] = jnp.where(jv == c, 1.0, chosen)

        qb = qf.astype(bf16)
        m_ref[...] = jnp.full(m_ref.shape, NEG, f32)
        l_ref[...] = jnp.zeros(l_ref.shape, f32)
        acc_ref[...] = jnp.zeros(acc_ref.shape, f32)

        def tile(i, carry):
            j = c - i
            tab = jnp.minimum(i, 2)
            k0 = pl.multiple_of(j * MOBA_BLOCK, MOBA_BLOCK)
            kj = kb_ref[hh, pl.ds(k0, MOBA_BLOCK), :]
            s = lax.dot_general(kj, qb, NT_DIMS, preferred_element_type=f32)
            lg = s * scale + bias_ref[hh, tab]
            lg = jnp.where(sel_ref[j] > 0.5, lg, NEG)
            m_old = m_ref[...]
            m_new = jnp.maximum(m_old, jnp.max(lg, axis=0, keepdims=True))
            p = jnp.exp(lg - m_new)
            alpha = jnp.exp(m_old - m_new)
            l_ref[...] = alpha * l_ref[...] + jnp.sum(p, axis=0, keepdims=True)
            vtj = vt_ref[hh, :, pl.ds(k0, MOBA_BLOCK)]
            acc_ref[...] = alpha * acc_ref[...] + jnp.dot(vtj, p.astype(bf16), preferred_element_type=f32)
            m_ref[...] = m_new
            return carry

        lax.fori_loop(0, c + 1, tile, 0)
        o_ref[:, lanes] = (acc_ref[...] / l_ref[...]).T.astype(o_ref.dtype)


def moba_prompt(proj, bias_tabs, *, n_seq, seq_len, n_heads, head_dim, q_col0):
    width = n_heads * head_dim
    gw = MOBA_HEAD_GROUP * head_dim
    n_groups = n_heads // MOBA_HEAD_GROUP
    n_blocks = seq_len // MOBA_BLOCK
    qb0, kb0, vb0 = q_col0 // gw, (q_col0 + width) // gw, (q_col0 + 2 * width) // gw
    est = 2 * (2 * _nbytes((MOBA_BLOCK, gw), f32) + 2 * _nbytes((seq_len, gw), f32)
               + _nbytes((MOBA_HEAD_GROUP, 3, MOBA_BLOCK, MOBA_BLOCK), f32)) \
        + 2 * _nbytes((MOBA_HEAD_GROUP, seq_len, head_dim), bf16) + 8 * _nbytes((MOBA_BLOCK, MOBA_BLOCK), f32)
    kern = functools.partial(_moba_prompt_kernel, head_dim=head_dim)
    return pl.pallas_call(
        kern,
        out_shape=jax.ShapeDtypeStruct((n_seq * seq_len, width), bf16),
        grid=(n_seq, n_groups, n_blocks),
        in_specs=[
            pl.BlockSpec((MOBA_BLOCK, gw), lambda b, g, c: (b * n_blocks + c, qb0 + g)),
            pl.BlockSpec((seq_len, gw), lambda b, g, c: (b, kb0 + g)),
            pl.BlockSpec((seq_len, gw), lambda b, g, c: (b, vb0 + g)),
            pl.BlockSpec((MOBA_HEAD_GROUP, 3, MOBA_BLOCK, MOBA_BLOCK), lambda b, g, c: (g, 0, 0, 0)),
        ],
        out_specs=pl.BlockSpec((MOBA_BLOCK, gw), lambda b, g, c: (b * n_blocks + c, g)),
        scratch_shapes=[
            pltpu.VMEM((MOBA_HEAD_GROUP, seq_len, head_dim), bf16),
            pltpu.VMEM((MOBA_HEAD_GROUP, head_dim, seq_len), bf16),
            pltpu.VMEM((MOBA_HEAD_GROUP, n_blocks, head_dim), f32),
            pltpu.VMEM((n_blocks, 1, MOBA_BLOCK), f32),
            pltpu.VMEM((1, MOBA_BLOCK), f32),
            pltpu.VMEM((1, MOBA_BLOCK), f32),
            pltpu.VMEM((head_dim, MOBA_BLOCK), f32),
        ],
        compiler_params=_params(("parallel", "parallel", "arbitrary"), est),
        name="moba_prompt",
    )(proj, proj, proj, bias_tabs)


def _moba_sample_kernel(pt_ref, q_ref, kn_ref, vn_ref, kc_ref, vc_ref, bias_ref, o_ref,
                        qbd_ref, pad_ref, sum_ref, m_ref, l_ref, part_ref,
                        *, n_heads, head_dim, n_q, n_near, n_pages, pages_per_block):
    del pt_ref
    p = pl.program_id(1)
    page = kc_ref.shape[0]
    n_cols = n_heads * n_q
    scale = head_dim ** -0.5

    @pl.when(p == 0)
    def _():
        qb = q_ref[...].astype(bf16)
        expand = jnp.where(lax.broadcasted_iota(jnp.int32, (n_q, n_cols), 1) % n_q
                           == lax.broadcasted_iota(jnp.int32, (n_q, n_cols), 0), 1.0, 0.0).astype(bf16)
        rep = lax.dot_general(qb, expand, TN_DIMS, preferred_element_type=f32)
        rh = lax.broadcasted_iota(jnp.int32, rep.shape, 0) // head_dim
        ch = lax.broadcasted_iota(jnp.int32, rep.shape, 1) // n_q
        qbd_ref[...] = jnp.where(rh == ch, rep, 0.0).astype(bf16)
        m_ref[...] = jnp.full(m_ref.shape, NEG, f32)
        l_ref[...] = jnp.zeros(l_ref.shape, f32)
        sum_ref[...] = jnp.zeros(sum_ref.shape, f32)

    def partial_softmax(k_blk, v_blk, bias, with_sum):
        lt = jnp.dot(k_blk.astype(bf16), qbd_ref[...], preferred_element_type=f32)
        if with_sum:
            sum_ref[pl.ds(p, 1), :] = jnp.sum(lt, axis=0, keepdims=True)
        lg = lt * scale + bias
        mp = jnp.max(lg, axis=0, keepdims=True)
        pp = jnp.exp(lg - mp)
        m_ref[pl.ds(p, 1), :] = mp
        l_ref[pl.ds(p, 1), :] = jnp.sum(pp, axis=0, keepdims=True)
        full = lax.dot_general(pp.astype(bf16), v_blk.astype(bf16), TN_DIMS, preferred_element_type=f32)
        for h in range(n_heads):
            part_ref[p, h * n_q:(h + 1) * n_q, :] = full[h * n_q:(h + 1) * n_q, h * head_dim:(h + 1) * head_dim]

    @pl.when(p < n_pages)
    def _():
        near = p - (n_pages - n_near)
        bias = bias_ref[jnp.maximum(near + 1, 0)]
        partial_softmax(kc_ref[...], vc_ref[...], bias, True)

    @pl.when(p == n_pages)
    def _():
        pad_ref[...] = jnp.zeros(pad_ref.shape, f32)
        pad_ref[0, 0:n_q, :] = kn_ref[...]
        pad_ref[1, 0:n_q, :] = vn_ref[...]
        partial_softmax(pad_ref[0], pad_ref[1], bias_ref[n_near + 1], False)

        sums = sum_ref[...]
        rowi = lax.broadcasted_iota(jnp.int32, sums.shape, 0)
        score = sums
        for s in range(1, pages_per_block):
            fwd = pltpu.roll(sums, sums.shape[0] - s, 0)
            bwd = pltpu.roll(sums, pages_per_block - s, 0)
            score = score + jnp.where(rowi % pages_per_block + s < pages_per_block, fwd, bwd)
        sel = jnp.zeros(sums.shape, f32)
        for _ in range(MOBA_TOPK):
            mx = jnp.max(score, axis=0, keepdims=True)
            first = jnp.min(jnp.where(score == mx, rowi, n_pages), axis=0, keepdims=True)
            hit = (rowi // pages_per_block) == (first // pages_per_block)
            sel = jnp.where(hit, 1.0, sel)
            score = jnp.where(hit, -jnp.inf, score)

        n_rows = m_ref.shape[0]
        m_all, l_all = m_ref[...], l_ref[...]
        rows = lax.broadcasted_iota(jnp.int32, m_all.shape, 0)
        sel_all = jnp.concatenate([sel, jnp.zeros((n_rows - n_pages, n_cols), f32)], axis=0)
        sel_all = jnp.where(rows == n_pages, 1.0, sel_all)
        m_top = jnp.max(jnp.where(sel_all > 0.5, m_all, NEG), axis=0, keepdims=True)
        w = jnp.where(sel_all > 0.5, jnp.exp(m_all - m_top), 0.0)
        denom = jnp.sum(w * l_all, axis=0, keepdims=True)
        wn = w / denom
        wn_sq = jnp.concatenate([wn, jnp.zeros((n_cols - n_rows, n_cols), f32)], axis=0)
        wt = wn_sq.T
        out = jnp.zeros((n_cols, head_dim), f32)
        for i in range(n_pages + 1):
            out = out + wt[:, i:i + 1] * part_ref[i]
        for h in range(n_heads):
            o_ref[:, h * head_dim:(h + 1) * head_dim] = out[h * n_q:(h + 1) * n_q, :].astype(o_ref.dtype)


def moba_sample(proj, cache_k, cache_v, page_table, bias_tabs, *, n_seq, n_q, n_heads, head_dim, q_col0, n_near):
    width = n_heads * head_dim
    n_cols = n_heads * n_q
    page = cache_k.shape[1]
    n_pages = page_table.shape[1]
    pages_per_block = MOBA_BLOCK // page
    qb0 = q_col0 // width
    n_rows = -(-(n_pages + 1) // V7X_SUBLANES) * V7X_SUBLANES
    assert n_cols == V7X_LANES and page == V7X_LANES and n_rows <= n_cols
    est = 2 * (3 * _nbytes((n_q, width), f32) + 2 * _nbytes((page, width), f32)
               + _nbytes(bias_tabs.shape, f32) + _nbytes((n_q, width), f32)) \
        + _nbytes((width, n_cols), bf16) + 2 * _nbytes((page, width), f32) \
        + _nbytes((n_pages + 1, n_cols, head_dim), f32) + 6 * _nbytes((page, width), f32)
    kern = functools.partial(_moba_sample_kernel, n_heads=n_heads, head_dim=head_dim, n_q=n_q,
                             n_near=n_near, n_pages=n_pages, pages_per_block=pages_per_block)

    def page_idx(n, p, pt):
        return (pt[n, jnp.minimum(p, n_pages - 1)], 0, 0)

    grid_spec = pltpu.PrefetchScalarGridSpec(
        num_scalar_prefetch=1,
        grid=(n_seq, n_pages + 1),
        in_specs=[
            pl.BlockSpec((n_q, width), lambda n, p, pt: (n, qb0)),
            pl.BlockSpec((n_q, width), lambda n, p, pt: (n, qb0 + 1)),
            pl.BlockSpec((n_q, width), lambda n, p, pt: (n, qb0 + 2)),
            pl.BlockSpec((None, page, width), page_idx),
            pl.BlockSpec((None, page, width), page_idx),
            pl.BlockSpec(bias_tabs.shape, lambda n, p, pt: (0, 0, 0)),
        ],
        out_specs=pl.BlockSpec((n_q, width), lambda n, p, pt: (n, 0)),
        scratch_shapes=[
            pltpu.VMEM((width, n_cols), bf16),
            pltpu.VMEM((2, page, width), f32),
            pltpu.VMEM((n_pages, n_cols), f32),
            pltpu.VMEM((n_rows, n_cols), f32),
            pltpu.VMEM((n_rows, n_cols), f32),
            pltpu.VMEM((n_pages + 1, n_cols, head_dim), f32),
        ],
    )
    return pl.pallas_call(
        kern,
        out_shape=jax.ShapeDtypeStruct((n_seq * n_q, width), f32),
        grid_spec=grid_spec,
        compiler_params=_params(("parallel", "arbitrary"), est),
        name="moba_sample",
    )(page_table, proj, proj, proj, cache_k, cache_v, bias_tabs)


def _peer_score_kernel(q_ref, sk_ref, s1_ref, s2_ref, e1_ref, e2_ref, tau_ref, *, n_heads, n_keys):
    dk = sk_ref.shape[3]
    neg_inf = -jnp.inf
    pairs = [(i, j) for i in range(PEER_TOPK) for j in range(PEER_TOPK) if (i + 1) * (j + 1) <= PEER_TOPK]
    n_cand = -(-len(pairs) // V7X_SUBLANES) * V7X_SUBLANES
    for h in range(n_heads):
        scores, tops = [], []
        for part in range(2):
            c0 = (2 * h + part) * dk
            s = lax.dot_general(sk_ref[h, part], q_ref[:, c0:c0 + dk], NT_DIMS, preferred_element_type=f32)
            scores.append(s)
            vals, w = [], s
            for _ in range(PEER_TOPK):
                mx = jnp.max(w, axis=0, keepdims=True)
                vals.append(mx)
                w = jnp.where(w == mx, neg_inf, w)
            tops.append(vals)
        a, b = tops
        tm = a[0].shape[1]
        cand = jnp.concatenate([a[i] + b[j] for i, j in pairs]
                               + [jnp.full((n_cand - len(pairs), tm), neg_inf, f32)], axis=0)
        rowi = lax.broadcasted_iota(jnp.int32, cand.shape, 0)
        best = []
        for _ in range(PEER_TOPK):
            mx = jnp.max(cand, axis=0, keepdims=True)
            first = jnp.min(jnp.where(cand == mx, rowi, n_cand), axis=0, keepdims=True)
            best.append(mx)
            cand = jnp.where(rowi == first, neg_inf, cand)
        z = jnp.zeros_like(best[0])
        for c in best:
            z = z + jnp.exp(c - best[0])
        s1_ref[h] = scores[0]
        s2_ref[h] = scores[1]
        e1_ref[h] = jnp.exp(scores[0] - a[0])
        e2_ref[h] = jnp.exp(scores[1] - b[0]) / z
        tau_ref[h:h + 1, :] = best[PEER_TOPK - 1]


def peer_scores(qp, subkeys, *, tm):
    t = qp.shape[0]
    n_heads, _, n_keys, dk = subkeys.shape
    tm = min(tm, t)
    big = jax.ShapeDtypeStruct((n_heads, n_keys, t), f32)
    blk = pl.BlockSpec((n_heads, n_keys, tm), lambda i: (0, 0, i))
    est = 2 * (_nbytes((tm, qp.shape[1]), bf16) + _nbytes(subkeys.shape, bf16) + 4 * _nbytes((n_heads, n_keys, tm), f32)) \
        + 16 * _nbytes((n_keys, tm), f32)
    kern = functools.partial(_peer_score_kernel, n_heads=n_heads, n_keys=n_keys)
    return pl.pallas_call(
        kern,
        out_shape=(big, big, big, big, jax.ShapeDtypeStruct((n_heads, t), f32)),
        grid=(t // tm,),
        in_specs=[pl.BlockSpec((tm, qp.shape[1]), lambda i: (i, 0)),
                  pl.BlockSpec(subkeys.shape, lambda i: (0, 0, 0, 0))],
        out_specs=(blk, blk, blk, blk, pl.BlockSpec((n_heads, tm), lambda i: (0, i))),
        compiler_params=_params(("parallel",), est),
        name="peer_scores",
    )(qp, subkeys)


def _peer_dense_kernel(x_ref, u_ref, v_ref, s1_ref, s2_ref, e1_ref, e2_ref, tau_ref, h_ref, g_ref, o_ref,
                       acc_ref, *, n_heads, n_keys):
    e = pl.program_id(1)
    te = u_ref.shape[0]
    rows_per_step = te // n_keys

    @pl.when(e == 0)
    def _():
        acc_ref[...] = jnp.zeros(acc_ref.shape, f32)

    act_t = lax.dot_general(u_ref[...], x_ref[...], NT_DIMS, preferred_element_type=f32)
    pieces = []
    for il in range(rows_per_step):
        i1 = e * rows_per_step + il
        act = jax.nn.gelu(act_t[il * n_keys:(il + 1) * n_keys, :])
        wt = jnp.zeros(act.shape, f32)
        for h in range(n_heads):
            pair = s2_ref[h] + s1_ref[h, pl.ds(i1, 1), :]
            wt = wt + jnp.where(pair >= tau_ref[h:h + 1, :], e2_ref[h] * e1_ref[h, pl.ds(i1, 1), :], 0.0)
        pieces.append((wt * act).astype(bf16))
    wact = pieces[0] if rows_per_step == 1 else jnp.concatenate(pieces, axis=0)
    acc_ref[...] += lax.dot_general(wact, v_ref[...], TN_DIMS, preferred_element_type=f32)

    @pl.when(e == pl.num_programs(1) - 1)
    def _():
        r = h_ref[...] + acc_ref[...]
        ms = jnp.mean(r * r, axis=-1, keepdims=True)
        o_ref[...] = r * lax.rsqrt(ms + EPS) * g_ref[...]


def peer_dense(xn, u_tab, v_tab, s1, s2, e1, e2, tau, h, g, *, tm, te):
    t, d = xn.shape
    n_exp = u_tab.shape[0]
    n_heads, n_keys, _ = s1.shape
    tm = min(tm, t)
    sblk = pl.BlockSpec((n_heads, n_keys, tm), lambda i, e: (0, 0, i))
    est = 2 * (_nbytes((tm, d), bf16) + 2 * _nbytes((te, d), bf16) + 4 * _nbytes((n_heads, n_keys, tm), f32)
               + 2 * _nbytes((tm, d), f32)) + _nbytes((tm, d), f32) + 6 * _nbytes((te, tm), f32)
    kern = functools.partial(_peer_dense_kernel, n_heads=n_heads, n_keys=n_keys)
    return pl.pallas_call(
        kern,
        out_shape=jax.ShapeDtypeStruct((t, d), f32),
        grid=(t // tm, n_exp // te),
        in_specs=[
            pl.BlockSpec((tm, d), lambda i, e: (i, 0)),
            pl.BlockSpec((te, d), lambda i, e: (e, 0)),
            pl.BlockSpec((te, d), lambda i, e: (e, 0)),
            sblk, sblk, sblk, sblk,
            pl.BlockSpec((n_heads, tm), lambda i, e: (0, i)),
            pl.BlockSpec((tm, d), lambda i, e: (i, 0)),
            pl.BlockSpec((1, d), lambda i, e: (0, 0)),
        ],
        out_specs=pl.BlockSpec((tm, d), lambda i, e: (i, 0)),
        scratch_shapes=[pltpu.VMEM((tm, d), f32)],
        compiler_params=_params(("parallel", "arbitrary"), est),
        name="peer_dense",
    )(xn, u_tab, v_tab, s1, s2, e1, e2, tau, h, g.reshape(1, d))


def _layer(x, prefix, attend, conv_tt, conv_dtype, lw, norm_final_g, *, n_seq, seq_len, peer_tm, peer_te):
    (norm_mix_g, w_in, b_gate, conv_w_dw, conv_b_dw, conv_ln_g, conv_ln_b, w_conv_out, w_attn_out, w_out,
     norm_ffn_g, peer_wq, peer_subkeys, peer_u, peer_v) = lw
    d = x.shape[1]
    conv_ch = conv_w_dw.shape[1]
    attn_w = w_attn_out.shape[0]
    q_col0 = 2 * conv_ch
    gate_col0 = q_col0 + 3 * attn_w

    xn = rmsnorm(x, norm_mix_g)
    proj = matmul(xn, w_in)
    hc, conv_state = conformer_conv(proj, prefix, conv_w_dw, conv_b_dw, conv_ln_g, conv_ln_b,
                                    n_seq=n_seq, seq_len=seq_len, tt=conv_tt, out_dtype=conv_dtype)
    attn = attend(proj, q_col0)
    merged = gated_merge(hc.astype(bf16), attn.astype(bf16), w_conv_out, w_attn_out, proj,
                         b_gate.reshape(1, -1), gate_col0)
    h = matmul(merged, w_out, residual=x)
    xn2 = rmsnorm(h, norm_ffn_g)
    qp = matmul(xn2, peer_wq, out_dtype=bf16)
    s1, s2, e1, e2, tau = peer_scores(qp, peer_subkeys, tm=peer_tm)
    y = peer_dense(xn2, peer_u, peer_v, s1, s2, e1, e2, tau, h, norm_final_g, tm=peer_tm, te=peer_te)
    k = proj[:, q_col0 + attn_w:q_col0 + 2 * attn_w]
    v = proj[:, q_col0 + 2 * attn_w:q_col0 + 3 * attn_w]
    return y, k, v, conv_state


def kernel(x_prompt, x_sample, cache_k, cache_v, state_conv, page_table, rel_bias, norm_mix_g, w_in, b_gate,
           conv_w_dw, conv_b_dw, conv_ln_g, conv_ln_b, w_conv_out, w_attn_out, w_out, norm_ffn_g,
           peer_wq, peer_subkeys, peer_u, peer_v, norm_final_g):
    depth = w_in.shape[0]
    assert depth == 1, "the final norm is fused into the last layer's PEER kernel"
    n_p, seq_p, d = x_prompt.shape
    n_s, seq_s, _ = x_sample.shape
    _, n_pool, page, n_heads, head_dim = cache_k.shape
    width = n_heads * head_dim
    past_len = page_table.shape[1] * page
    conv_w = conv_w_dw.shape[1]
    conv_ch = conv_w_dw.shape[2]

    n_near = sum(1 for pg in range(past_len // page) if past_len - (pg * page + page - 1) < MAX_DISTANCE)

    l = 0
    lw = (norm_mix_g[l], w_in[l].astype(bf16), b_gate[l], conv_w_dw[l], conv_b_dw[l], conv_ln_g[l], conv_ln_b[l],
          w_conv_out[l].astype(bf16), w_attn_out[l].astype(bf16), w_out[l].astype(bf16), norm_ffn_g[l],
          peer_wq[l].astype(bf16), peer_subkeys[l].astype(bf16), peer_u[l].astype(bf16), peer_v[l].astype(bf16))

    p_tabs = prompt_bias_tables(rel_bias)
    s_tabs = sample_bias_tables(rel_bias, past_len=past_len, page=page, n_q=seq_s, n_near=n_near)

    attend_p = lambda proj, q_col0: moba_prompt(proj, p_tabs, n_seq=n_p, seq_len=seq_p, n_heads=n_heads,
                                                head_dim=head_dim, q_col0=q_col0)
    ck = cache_k[l].reshape(n_pool, page, width)
    cv = cache_v[l].reshape(n_pool, page, width)
    attend_s = lambda proj, q_col0: moba_sample(proj, ck, cv, page_table, s_tabs, n_seq=n_s, n_q=seq_s,
                                                n_heads=n_heads, head_dim=head_dim, q_col0=q_col0, n_near=n_near)

    zeros_prefix = jnp.zeros((n_p, conv_w - 1, conv_ch), f32)
    yp, kp, vp, cp = _layer(x_prompt.reshape(n_p * seq_p, d), zeros_prefix, attend_p, 256, bf16, lw, norm_final_g,
                            n_seq=n_p, seq_len=seq_p, peer_tm=256, peer_te=512)
    ys, ks, vs, cs = _layer(x_sample.reshape(n_s * seq_s, d), state_conv[l], attend_s, seq_s, f32, lw, norm_final_g,
                            n_seq=n_s, seq_len=seq_s, peer_tm=256, peer_te=512)

    return (yp.reshape(n_p, seq_p, d), ys.reshape(n_s, seq_s, d),
            kp.reshape(1, n_p, seq_p, n_heads, head_dim), vp.reshape(1, n_p, seq_p, n_heads, head_dim),
            ks.reshape(1, n_s, seq_s, n_heads, head_dim), vs.reshape(1, n_s, seq_s, n_heads, head_dim),
            cp[None], cs[None])
```

```python
import functools
import math

import jax
import jax.numpy as jnp
from jax import lax
from jax.experimental import pallas as pl
from jax.experimental.pallas import tpu as pltpu

f32 = jnp.float32
bf16 = jnp.bfloat16

MOBA_BLOCK = 256
MOBA_TOPK = 3
NUM_BUCKETS = 32
MAX_DISTANCE = 128
PEER_TOPK = 16
EPS = 1e-6
NEG = -1e30

V7X_VMEM_BYTES = 64 * 1024 * 1024
V7X_LANES = 128
V7X_SUBLANES = 8
V7X_MXU_WIDTH = 256
VMEM_CAP_BYTES = V7X_VMEM_BYTES - 8 * 1024 * 1024

NT_DIMS = (((1,), (1,)), ((), ()))
TN_DIMS = (((0,), (0,)), ((), ()))


def _params(semantics, est_bytes):
    limit = int(min(max(est_bytes, 16 * 1024 * 1024), VMEM_CAP_BYTES))
    return pltpu.CompilerParams(dimension_semantics=semantics, vmem_limit_bytes=limit)


def _nbytes(shape, dtype):
    return math.prod(shape) * jnp.dtype(dtype).itemsize


def _rmsnorm_kernel(x_ref, g_ref, o_ref):
    x = x_ref[...]
    ms = jnp.mean(x * x, axis=-1, keepdims=True)
    o_ref[...] = (x * lax.rsqrt(ms + EPS) * g_ref[...]).astype(o_ref.dtype)


def rmsnorm(x, g, out_dtype=bf16):
    t, d = x.shape
    tr = min(512, t)
    est = 2 * (_nbytes((tr, d), f32) + _nbytes((tr, d), out_dtype)) + 4 * _nbytes((tr, d), f32)
    return pl.pallas_call(
        _rmsnorm_kernel,
        out_shape=jax.ShapeDtypeStruct((t, d), out_dtype),
        grid=(t // tr,),
        in_specs=[pl.BlockSpec((tr, d), lambda i: (i, 0)), pl.BlockSpec((1, d), lambda i: (0, 0))],
        out_specs=pl.BlockSpec((tr, d), lambda i: (i, 0)),
        compiler_params=_params(("parallel",), est),
        name="rmsnorm",
    )(x, g.reshape(1, d))


def _mm_kernel(x_ref, w_ref, o_ref):
    o_ref[...] = jnp.dot(x_ref[...], w_ref[...], preferred_element_type=f32).astype(o_ref.dtype)


def _mm_res_kernel(x_ref, w_ref, r_ref, o_ref):
    acc = jnp.dot(x_ref[...], w_ref[...], preferred_element_type=f32)
    o_ref[...] = (r_ref[...] + acc).astype(o_ref.dtype)


def matmul(x, w, *, out_dtype=f32, residual=None, tm=512, tn=1024):
    m, k = x.shape
    n = w.shape[1]
    tm, tn = min(tm, m), min(tn, n)
    est = 2 * (_nbytes((tm, k), bf16) + _nbytes((k, tn), bf16) + _nbytes((tm, tn), out_dtype)
               + _nbytes((tm, tn), f32)) + _nbytes((tm, tn), f32)
    in_specs = [pl.BlockSpec((tm, k), lambda i, j: (i, 0)), pl.BlockSpec((k, tn), lambda i, j: (0, j))]
    args = [x, w]
    body = _mm_kernel
    if residual is not None:
        in_specs.append(pl.BlockSpec((tm, tn), lambda i, j: (i, j)))
        args.append(residual)
        body = _mm_res_kernel
    return pl.pallas_call(
        body,
        out_shape=jax.ShapeDtypeStruct((m, n), out_dtype),
        grid=(m // tm, n // tn),
        in_specs=in_specs,
        out_specs=pl.BlockSpec((tm, tn), lambda i, j: (i, j)),
        compiler_params=_params(("parallel", "parallel"), est),
        name="matmul",
    )(*args)


def _merge_kernel(hc_ref, at_ref, wc_ref, wa_ref, ga_ref, gb_ref, ba_ref, bb_ref, o_ref):
    conv_out = jnp.dot(hc_ref[...], wc_ref[...], preferred_element_type=f32)
    attn_out = jnp.dot(at_ref[...], wa_ref[...], preferred_element_type=f32)
    gate_a = jax.nn.sigmoid(ga_ref[...] + ba_ref[...])
    gate_b = jax.nn.sigmoid(gb_ref[...] + bb_ref[...])
    o_ref[...] = (gate_a * conv_out + gate_b * attn_out).astype(o_ref.dtype)


def gated_merge(hc, attn, w_conv_out, w_attn_out, proj, b_gate, gate_col0, *, tm=512, tn=512):
    m, k = hc.shape
    n = w_conv_out.shape[1]
    tm, tn = min(tm, m), min(tn, n)
    ga0, gb0 = gate_col0 // tn, (gate_col0 + n) // tn
    nb = n // tn
    est = 2 * (2 * _nbytes((tm, k), bf16) + 2 * _nbytes((k, tn), bf16) + 3 * _nbytes((tm, tn), f32)) \
        + 4 * _nbytes((tm, tn), f32)
    return pl.pallas_call(
        _merge_kernel,
        out_shape=jax.ShapeDtypeStruct((m, n), bf16),
        grid=(m // tm, nb),
        in_specs=[
            pl.BlockSpec((tm, k), lambda i, j: (i, 0)),
            pl.BlockSpec((tm, k), lambda i, j: (i, 0)),
            pl.BlockSpec((k, tn), lambda i, j: (0, j)),
            pl.BlockSpec((k, tn), lambda i, j: (0, j)),
            pl.BlockSpec((tm, tn), lambda i, j: (i, ga0 + j)),
            pl.BlockSpec((tm, tn), lambda i, j: (i, gb0 + j)),
            pl.BlockSpec((1, tn), lambda i, j: (0, j)),
            pl.BlockSpec((1, tn), lambda i, j: (0, nb + j)),
        ],
        out_specs=pl.BlockSpec((tm, tn), lambda i, j: (i, j)),
        compiler_params=_params(("parallel", "parallel"), est),
        name="gated_merge",
    )(hc, attn, w_conv_out, w_attn_out, proj, proj, b_gate, b_gate)


CONV_ROW_CHUNK = 32
CONV_LANE_CHUNK = 256
CONV_HALO = 32


def _conv_kernel(a_ref, gt_ref, pre_ref, wdw_ref, bdw_ref, lng_ref, lnb_ref, hc_ref, st_ref, ubuf, ybuf,
                 *, tt, width):
    t = pl.program_id(1)
    c = ubuf.shape[1]
    lead = CONV_HALO - (width - 1)
    rc = min(CONV_ROW_CHUNK, tt)

    @pl.when(t == 0)
    def _():
        ubuf[0:lead, :] = jnp.zeros((lead, c), f32)
        ubuf[lead:CONV_HALO, :] = pre_ref[...]

    ubuf[CONV_HALO:CONV_HALO + tt, :] = a_ref[...] * jax.nn.sigmoid(gt_ref[...])

    n_lane = c // CONV_LANE_CHUNK

    def chunk(idx, carry):
        r0 = pl.multiple_of((idx // n_lane) * rc, rc)
        c0 = pl.multiple_of((idx % n_lane) * CONV_LANE_CHUNK, CONV_LANE_CHUNK)
        win = ubuf[pl.ds(r0, rc + CONV_HALO), pl.ds(c0, CONV_LANE_CHUNK)]
        acc = jnp.broadcast_to(bdw_ref[:, pl.ds(c0, CONV_LANE_CHUNK)], (rc, CONV_LANE_CHUNK))
        for phase in range(V7X_SUBLANES):
            offs = [o for o in range(lead, lead + width) if o % V7X_SUBLANES == phase]
            if not offs:
                continue
            span = offs[-1] - phase + rc
            shifted = win[phase:phase + span]
            for o in offs:
                q = o - phase
                w_row = wdw_ref[o - lead:o - lead + 1, pl.ds(c0, CONV_LANE_CHUNK)]
                acc = acc + shifted[q:q + rc] * w_row
        ybuf[pl.ds(r0, rc), pl.ds(c0, CONV_LANE_CHUNK)] = acc
        return carry

    lax.fori_loop(0, (tt // rc) * n_lane, chunk, 0)

    def ln_chunk(i, carry):
        r0 = pl.multiple_of(i * rc, rc)
        y = ybuf[pl.ds(r0, rc), :]
        mu = jnp.mean(y, axis=-1, keepdims=True)
        yc = y - mu
        var = jnp.mean(yc * yc, axis=-1, keepdims=True)
        yn = yc * lax.rsqrt(var + EPS) * lng_ref[...] + lnb_ref[...]
        hc_ref[pl.ds(r0, rc), :] = (yn * jax.nn.sigmoid(yn)).astype(hc_ref.dtype)
        return carry

    lax.fori_loop(0, tt // rc, ln_chunk, 0)

    @pl.when(t == pl.num_programs(1) - 1)
    def _():
        st_ref[...] = ubuf[tt + lead:tt + CONV_HALO, :]

    ubuf[0:CONV_HALO, :] = ubuf[tt:tt + CONV_HALO, :]


def conformer_conv(proj, prefix, w_dw, b_dw, ln_g, ln_b, *, n_seq, seq_len, tt, out_dtype):
    width, c = w_dw.shape
    nt = seq_len // tt
    est = 2 * (2 * _nbytes((tt, c), f32) + _nbytes((width - 1, c), f32) + _nbytes((width, c), f32)
               + _nbytes((tt, c), out_dtype) + _nbytes((width - 1, c), f32)) \
        + _nbytes((2 * tt + CONV_HALO, c), f32) + 8 * _nbytes((CONV_ROW_CHUNK, c), f32)
    kern = functools.partial(_conv_kernel, tt=tt, width=width)
    return pl.pallas_call(
        kern,
        out_shape=(jax.ShapeDtypeStruct((n_seq * seq_len, c), out_dtype),
                   jax.ShapeDtypeStruct((n_seq, width - 1, c), f32)),
        grid=(n_seq, nt),
        in_specs=[
            pl.BlockSpec((tt, c), lambda b, t: (b * nt + t, 0)),
            pl.BlockSpec((tt, c), lambda b, t: (b * nt + t, 1)),
            pl.BlockSpec((None, width - 1, c), lambda b, t: (b, 0, 0)),
            pl.BlockSpec((width, c), lambda b, t: (0, 0)),
            pl.BlockSpec((1, c), lambda b, t: (0, 0)),
            pl.BlockSpec((1, c), lambda b, t: (0, 0)),
            pl.BlockSpec((1, c), lambda b, t: (0, 0)),
        ],
        out_specs=(pl.BlockSpec((tt, c), lambda b, t: (b * nt + t, 0)),
                   pl.BlockSpec((None, width - 1, c), lambda b, t: (b, 0, 0))),
        scratch_shapes=[pltpu.VMEM((tt + CONV_HALO, c), f32), pltpu.VMEM((tt, c), f32)],
        compiler_params=_params(("parallel", "arbitrary"), est),
        name="conformer_conv",
    )(proj, proj, prefix, w_dw, b_dw.reshape(1, c), ln_g.reshape(1, c), ln_b.reshape(1, c))


def _rel_bucket(dist):
    n = jnp.maximum(dist, 0)
    max_exact = NUM_BUCKETS // 2
    large = max_exact + (jnp.log(jnp.maximum(n, 1).astype(f32) / max_exact)
                         / math.log(MAX_DISTANCE / max_exact) * (NUM_BUCKETS - max_exact)).astype(jnp.int32)
    large = jnp.minimum(large, NUM_BUCKETS - 1)
    return jnp.where(n < max_exact, n, large)


def _bias_from_dist(dist, lookup):
    bucket = _rel_bucket(dist)
    out = jnp.zeros(dist.shape, f32)
    for b in range(NUM_BUCKETS):
        out = jnp.where(bucket == b, lookup(b), out)
    return jnp.where(dist >= 0, out, NEG)


def _prompt_bias_kernel(rb_ref, o_ref):
    h = pl.program_id(0)
    ki = lax.broadcasted_iota(jnp.int32, (MOBA_BLOCK, MOBA_BLOCK), 0)
    qi = lax.broadcasted_iota(jnp.int32, (MOBA_BLOCK, MOBA_BLOCK), 1)
    for tab in range(3):
        dist = tab * MOBA_BLOCK + qi - ki
        o_ref[tab] = _bias_from_dist(dist, lambda b: rb_ref[b, h])


def prompt_bias_tables(rel_bias):
    n_heads = rel_bias.shape[1]
    return pl.pallas_call(
        _prompt_bias_kernel,
        out_shape=jax.ShapeDtypeStruct((n_heads, 3, MOBA_BLOCK, MOBA_BLOCK), f32),
        grid=(n_heads,),
        in_specs=[pl.BlockSpec(memory_space=pltpu.SMEM)],
        out_specs=pl.BlockSpec((None, 3, MOBA_BLOCK, MOBA_BLOCK), lambda h: (h, 0, 0, 0)),
        compiler_params=_params(("parallel",), 16 * 1024 * 1024),
        name="prompt_bias_tables",
    )(rel_bias)


def _sample_bias_kernel(rb_ref, page_ref, own_ref, *, past_len, page, n_q, n_near, n_heads):
    n_rows = n_heads * n_q
    rowh = lax.broadcasted_iota(jnp.int32, (n_rows, 1), 0) // n_q

    def lookup(b):
        out = jnp.zeros((n_rows, 1), f32)
        for hh in range(n_heads):
            out = jnp.where(rowh == hh, rb_ref[b, hh], out)
        return out

    def table(width, col_head, dist):
        row = lax.broadcasted_iota(jnp.int32, (n_rows, width), 0)
        col = lax.broadcasted_iota(jnp.int32, (n_rows, width), 1)
        same_head = (row // n_q) == col_head(col)
        return jnp.where(same_head, _bias_from_dist(dist(row % n_q, col), lookup), NEG)

    n_pages = past_len // page
    wide = page * n_heads
    page_ref[0] = table(wide, lambda c: c % n_heads, lambda q, c: jnp.full(c.shape, MAX_DISTANCE, jnp.int32))
    for i in range(n_near):
        pg = n_pages - n_near + i
        page_ref[1 + i] = table(wide, lambda c: c % n_heads,
                                lambda q, c: past_len + q - (pg * page + c // n_heads))
    own_ref[...] = table(n_rows, lambda c: c // n_q, lambda q, c: q - c % n_q)


def sample_bias_tables(rel_bias, *, past_len, page, n_q, n_near):
    n_heads = rel_bias.shape[1]
    n_rows = n_heads * n_q
    kern = functools.partial(_sample_bias_kernel, past_len=past_len, page=page, n_q=n_q, n_near=n_near,
                             n_heads=n_heads)
    return pl.pallas_call(
        kern,
        out_shape=(jax.ShapeDtypeStruct((n_near + 1, n_rows, page * n_heads), f32),
                   jax.ShapeDtypeStruct((n_rows, n_rows), f32)),
        in_specs=[pl.BlockSpec(memory_space=pltpu.SMEM)],
        out_specs=(pl.BlockSpec(memory_space=pltpu.VMEM), pl.BlockSpec(memory_space=pltpu.VMEM)),
        name="sample_bias_tables",
    )(rel_bias)


MOBA_HEAD_GROUP = 4


def _moba_prompt_kernel(q_ref, k_ref, v_ref, bias_ref, o_ref,
                        kb_ref, vt_ref, mean_ref, qb_ref, sel_ref, m_ref, l_ref, acc_ref,
                        s_ref, p_ref, alpha_ref, *, head_dim):
    c = pl.program_id(2)
    n_blocks = k_ref.shape[0] // MOBA_BLOCK
    scale = head_dim ** -0.5

    for hh in range(MOBA_HEAD_GROUP):
        lanes = slice(hh * head_dim, (hh + 1) * head_dim)

        @pl.when(c == 0)
        def _():
            kf = k_ref[:, lanes]
            kb_ref[hh] = kf.astype(bf16)
            vt_ref[hh] = v_ref[:, lanes].T.astype(bf16)
            mean_ref[hh] = jnp.mean(kf.reshape(n_blocks, MOBA_BLOCK, head_dim), axis=1)

        qf = q_ref[:, lanes]
        st = lax.dot_general(mean_ref[hh], qf, NT_DIMS, preferred_element_type=f32,
                             precision=lax.Precision.HIGHEST)
        row = lax.broadcasted_iota(jnp.int32, st.shape, 0)
        sm = jnp.where(row < c, st, NEG)
        for j in range(n_blocks):
            rj = sm[j:j + 1, :]
            beats = jnp.where((sm > rj) | ((sm == rj) & (row < j)), 1.0, 0.0)
            rank = jnp.sum(beats, axis=0, keepdims=True)
            jv = jnp.full(rank.shape, j, jnp.int32)
            chosen = jnp.where((rank < MOBA_TOPK) & (jv < c), 1.0, 0.0)
            sel_ref[hh, j] = jnp.where(jv == c, 1.0, chosen)

        qb_ref[hh] = qf.astype(bf16)
        m_ref[hh] = jnp.full(m_ref.shape[1:], NEG, f32)
        l_ref[hh] = jnp.zeros(l_ref.shape[1:], f32)
        acc_ref[hh] = jnp.zeros(acc_ref.shape[1:], f32)

    def tile(i, carry):
        j = c - i
        tab = jnp.minimum(i, 2)
        k0 = pl.multiple_of(j * MOBA_BLOCK, MOBA_BLOCK)
        for hh in range(MOBA_HEAD_GROUP):
            kj = kb_ref[hh, pl.ds(k0, MOBA_BLOCK), :]
            s_ref[hh] = lax.dot_general(kj, qb_ref[hh], NT_DIMS, preferred_element_type=f32)
        for hh in range(MOBA_HEAD_GROUP):
            lg = s_ref[hh] * scale + bias_ref[hh, tab]
            lg = jnp.where(sel_ref[hh, j] > 0.5, lg, NEG)
            m_old = m_ref[hh]
            m_new = jnp.maximum(m_old, jnp.max(lg, axis=0, keepdims=True))
            p = jnp.exp(lg - m_new)
            alpha = jnp.exp(m_old - m_new)
            l_ref[hh] = alpha * l_ref[hh] + jnp.sum(p, axis=0, keepdims=True)
            p_ref[hh] = p.astype(bf16)
            alpha_ref[hh] = alpha
            m_ref[hh] = m_new
        for hh in range(MOBA_HEAD_GROUP):
            vtj = vt_ref[hh, :, pl.ds(k0, MOBA_BLOCK)]
            acc_ref[hh] = alpha_ref[hh] * acc_ref[hh] + jnp.dot(vtj, p_ref[hh], preferred_element_type=f32)
        return carry

    lax.fori_loop(0, c + 1, tile, 0)
    for hh in range(MOBA_HEAD_GROUP):
        o_ref[:, hh * head_dim:(hh + 1) * head_dim] = (acc_ref[hh] / l_ref[hh]).T.astype(o_ref.dtype)


def moba_prompt(proj, bias_tabs, *, n_seq, seq_len, n_heads, head_dim, q_col0):
    width = n_heads * head_dim
    gw = MOBA_HEAD_GROUP * head_dim
    n_groups = n_heads // MOBA_HEAD_GROUP
    n_blocks = seq_len // MOBA_BLOCK
    qb0, kb0, vb0 = q_col0 // gw, (q_col0 + width) // gw, (q_col0 + 2 * width) // gw
    est = 2 * (2 * _nbytes((MOBA_BLOCK, gw), f32) + 2 * _nbytes((seq_len, gw), f32)
               + _nbytes((MOBA_HEAD_GROUP, 3, MOBA_BLOCK, MOBA_BLOCK), f32)) \
        + 2 * _nbytes((MOBA_HEAD_GROUP, seq_len, head_dim), bf16) + 8 * _nbytes((MOBA_BLOCK, MOBA_BLOCK), f32)
    kern = functools.partial(_moba_prompt_kernel, head_dim=head_dim)
    return pl.pallas_call(
        kern,
        out_shape=jax.ShapeDtypeStruct((n_seq * seq_len, width), bf16),
        grid=(n_seq, n_groups, n_blocks),
        in_specs=[
            pl.BlockSpec((MOBA_BLOCK, gw), lambda b, g, c: (b * n_blocks + c, qb0 + g)),
            pl.BlockSpec((seq_len, gw), lambda b, g, c: (b, kb0 + g)),
            pl.BlockSpec((seq_len, gw), lambda b, g, c: (b, vb0 + g)),
            pl.BlockSpec((MOBA_HEAD_GROUP, 3, MOBA_BLOCK, MOBA_BLOCK), lambda b, g, c: (g, 0, 0, 0)),
        ],
        out_specs=pl.BlockSpec((MOBA_BLOCK, gw), lambda b, g, c: (b * n_blocks + c, g)),
        scratch_shapes=[
            pltpu.VMEM((MOBA_HEAD_GROUP, seq_len, head_dim), bf16),
            pltpu.VMEM((MOBA_HEAD_GROUP, head_dim, seq_len), bf16),
            pltpu.VMEM((MOBA_HEAD_GROUP, n_blocks, head_dim), f32),
            pltpu.VMEM((MOBA_HEAD_GROUP, MOBA_BLOCK, head_dim), bf16),
            pltpu.VMEM((MOBA_HEAD_GROUP, n_blocks, 1, MOBA_BLOCK), f32),
            pltpu.VMEM((MOBA_HEAD_GROUP, 1, MOBA_BLOCK), f32),
            pltpu.VMEM((MOBA_HEAD_GROUP, 1, MOBA_BLOCK), f32),
            pltpu.VMEM((MOBA_HEAD_GROUP, head_dim, MOBA_BLOCK), f32),
            pltpu.VMEM((MOBA_HEAD_GROUP, MOBA_BLOCK, MOBA_BLOCK), f32),
            pltpu.VMEM((MOBA_HEAD_GROUP, MOBA_BLOCK, MOBA_BLOCK), bf16),
            pltpu.VMEM((MOBA_HEAD_GROUP, 1, MOBA_BLOCK), f32),
        ],
        compiler_params=_params(("parallel", "parallel", "arbitrary"), est),
        name="moba_prompt",
    )(proj, proj, proj, bias_tabs)


def _moba_sample_kernel(pt_ref, q_ref, kn_ref, vn_ref, *rest,
                        n_heads, head_dim, n_q, n_near, n_pages, pages_per_block):
    del pt_ref
    kc_refs = rest[:pages_per_block]
    vc_refs = rest[pages_per_block:2 * pages_per_block]
    tab_ref, own_ref, o_ref, qb_ref, qf_ref, own_kv_ref, score_ref, m_ref, l_ref, part_ref = rest[2 * pages_per_block:]
    s = pl.program_id(1)
    n_blocks = n_pages // pages_per_block
    page = kc_refs[0].shape[0]
    n_rows = n_heads * n_q
    scale = head_dim ** -0.5
    lane = lax.broadcasted_iota(jnp.int32, (n_rows, V7X_LANES), 1)

    @pl.when(s == 0)
    def _():
        for h in range(n_heads):
            qf_ref[h * n_q:(h + 1) * n_q, :] = q_ref[:, h * head_dim:(h + 1) * head_dim]
        qb_ref[...] = qf_ref[...].astype(bf16)
        score_ref[...] = jnp.full(score_ref.shape, -jnp.inf, f32)
        m_ref[...] = jnp.full(m_ref.shape, NEG, f32)
        l_ref[...] = jnp.zeros(l_ref.shape, f32)

    def record(ref, col):
        ref[...] = jnp.where(lane == s, col, ref[...])

    def partial_softmax(logits, values):
        m = functools.reduce(jnp.maximum, [jnp.max(lg, axis=1, keepdims=True) for lg in logits])
        ps = [jnp.exp(lg - m) for lg in logits]
        record(m_ref, m)
        record(l_ref, functools.reduce(jnp.add, [jnp.sum(p, axis=1, keepdims=True) for p in ps]))
        part_ref[s] = functools.reduce(
            jnp.add, [jnp.dot(p.astype(bf16), v, preferred_element_type=f32) for p, v in zip(ps, values)])

    @pl.when(s < n_blocks)
    def _():
        logits, values = [], []
        ksum = jnp.zeros((n_heads, head_dim), f32)
        for i in range(pages_per_block):
            k3 = kc_refs[i][...]
            ksum = ksum + jnp.sum(k3, axis=0)
            k2 = k3.reshape(page * n_heads, head_dim).astype(bf16)
            raw = lax.dot_general(qb_ref[...], k2, NT_DIMS, preferred_element_type=f32)
            near = s * pages_per_block + i - (n_pages - n_near)
            logits.append(raw * scale + tab_ref[jnp.maximum(near + 1, 0)])
            values.append(vc_refs[i][...].reshape(page * n_heads, head_dim).astype(bf16))
        ksum_rows = jnp.concatenate(
            [jnp.broadcast_to(ksum[h:h + 1, :], (n_q, head_dim)) for h in range(n_heads)], axis=0)
        record(score_ref, jnp.sum(qf_ref[...] * ksum_rows, axis=1, keepdims=True))
        partial_softmax(logits, values)

    @pl.when(s == n_blocks)
    def _():
        for h in range(n_heads):
            own_kv_ref[0, h * n_q:(h + 1) * n_q, :] = kn_ref[:, h * head_dim:(h + 1) * head_dim]
            own_kv_ref[1, h * n_q:(h + 1) * n_q, :] = vn_ref[:, h * head_dim:(h + 1) * head_dim]
        raw = lax.dot_general(qb_ref[...], own_kv_ref[0].astype(bf16), NT_DIMS, preferred_element_type=f32)
        partial_softmax([raw * scale + own_ref[...]], [own_kv_ref[1].astype(bf16)])

        score = score_ref[...]
        sel = jnp.zeros(score.shape, f32)
        for _ in range(MOBA_TOPK):
            mx = jnp.max(score, axis=1, keepdims=True)
            first = jnp.min(jnp.where(score == mx, lane, V7X_LANES), axis=1, keepdims=True)
            hit = lane == first
            sel = jnp.where(hit, 1.0, sel)
            score = jnp.where(hit, -jnp.inf, score)
        sel = jnp.where(lane == n_blocks, 1.0, sel)

        m_all, l_all = m_ref[...], l_ref[...]
        m_top = jnp.max(jnp.where(sel > 0.5, m_all, NEG), axis=1, keepdims=True)
        w = jnp.where(sel > 0.5, jnp.exp(m_all - m_top), 0.0)
        wn = w / jnp.sum(w * l_all, axis=1, keepdims=True)
        out = jnp.zeros((n_rows, head_dim), f32)
        for i in range(n_blocks + 1):
            out = out + wn[:, i:i + 1] * part_ref[i]
        for h in range(n_heads):
            o_ref[:, h * head_dim:(h + 1) * head_dim] = out[h * n_q:(h + 1) * n_q, :].astype(o_ref.dtype)


def moba_sample(proj, cache_k, cache_v, page_table, page_tabs, own_tab, *, layer, n_seq, n_q, q_col0, n_near):
    _, _, page, n_heads, head_dim = cache_k.shape
    width = n_heads * head_dim
    n_rows = n_heads * n_q
    n_pages = page_table.shape[1]
    pages_per_block = MOBA_BLOCK // page
    n_blocks = n_pages // pages_per_block
    qb0 = q_col0 // width
    assert n_rows == V7X_LANES and MOBA_TOPK <= n_blocks < V7X_LANES and n_pages % pages_per_block == 0
    page_bytes = _nbytes((page, n_heads, head_dim), f32)
    est = 2 * (4 * _nbytes((n_q, width), f32) + 2 * pages_per_block * page_bytes
               + _nbytes(page_tabs.shape, f32) + _nbytes(own_tab.shape, f32)) \
        + _nbytes((n_blocks + 1, n_rows, head_dim), f32) + 8 * pages_per_block * page_bytes
    kern = functools.partial(_moba_sample_kernel, n_heads=n_heads, head_dim=head_dim, n_q=n_q,
                             n_near=n_near, n_pages=n_pages, pages_per_block=pages_per_block)

    def page_spec(i):
        def index(n, s, pt):
            return (layer, pt[n, jnp.minimum(s * pages_per_block + i, n_pages - 1)], 0, 0, 0)
        return pl.BlockSpec((None, None, page, n_heads, head_dim), index)

    pages = [page_spec(i) for i in range(pages_per_block)]
    stat = pltpu.VMEM((n_rows, V7X_LANES), f32)
    grid_spec = pltpu.PrefetchScalarGridSpec(
        num_scalar_prefetch=1,
        grid=(n_seq, n_blocks + 1),
        in_specs=[
            pl.BlockSpec((n_q, width), lambda n, s, pt: (n, qb0)),
            pl.BlockSpec((n_q, width), lambda n, s, pt: (n, qb0 + 1)),
            pl.BlockSpec((n_q, width), lambda n, s, pt: (n, qb0 + 2)),
            *pages, *pages,
            pl.BlockSpec(page_tabs.shape, lambda n, s, pt: (0, 0, 0)),
            pl.BlockSpec(own_tab.shape, lambda n, s, pt: (0, 0)),
        ],
        out_specs=pl.BlockSpec((n_q, width), lambda n, s, pt: (n, 0)),
        scratch_shapes=[
            pltpu.VMEM((n_rows, head_dim), bf16),
            pltpu.VMEM((n_rows, head_dim), f32),
            pltpu.VMEM((2, n_rows, head_dim), f32),
            stat, stat, stat,
            pltpu.VMEM((n_blocks + 1, n_rows, head_dim), f32),
        ],
    )
    return pl.pallas_call(
        kern,
        out_shape=jax.ShapeDtypeStruct((n_seq * n_q, width), f32),
        grid_spec=grid_spec,
        compiler_params=_params(("parallel", "arbitrary"), est),
        name="moba_sample",
    )(page_table, proj, proj, proj, *([cache_k] * pages_per_block), *([cache_v] * pages_per_block),
      page_tabs, own_tab)


def _peer_score_kernel(q_ref, sk_ref, s1_ref, s2_ref, e1_ref, e2_ref, tau_ref, *, n_heads, n_keys):
    dk = sk_ref.shape[3]
    neg_inf = -jnp.inf
    pairs = [(i, j) for i in range(PEER_TOPK) for j in range(PEER_TOPK) if (i + 1) * (j + 1) <= PEER_TOPK]
    n_cand = -(-len(pairs) // V7X_SUBLANES) * V7X_SUBLANES
    for h in range(n_heads):
        scores, tops = [], []
        for part in range(2):
            c0 = (2 * h + part) * dk
            s = lax.dot_general(sk_ref[h, part], q_ref[:, c0:c0 + dk], NT_DIMS, preferred_element_type=f32)
            scores.append(s)
            vals, w = [], s
            for _ in range(PEER_TOPK):
                mx = jnp.max(w, axis=0, keepdims=True)
                vals.append(mx)
                w = jnp.where(w == mx, neg_inf, w)
            tops.append(vals)
        a, b = tops
        tm = a[0].shape[1]
        cand = jnp.concatenate([a[i] + b[j] for i, j in pairs]
                               + [jnp.full((n_cand - len(pairs), tm), neg_inf, f32)], axis=0)
        rowi = lax.broadcasted_iota(jnp.int32, cand.shape, 0)
        best = []
        for _ in range(PEER_TOPK):
            mx = jnp.max(cand, axis=0, keepdims=True)
            first = jnp.min(jnp.where(cand == mx, rowi, n_cand), axis=0, keepdims=True)
            best.append(mx)
            cand = jnp.where(rowi == first, neg_inf, cand)
        z = jnp.zeros_like(best[0])
        for c in best:
            z = z + jnp.exp(c - best[0])
        e1 = jnp.exp(scores[0] - a[0])
        e2 = jnp.exp(scores[1] - b[0]) / z
        for c in range(tm // V7X_LANES):
            cols = slice(c * V7X_LANES, (c + 1) * V7X_LANES)
            s1_ref[h, c] = scores[0][:, cols]
            s2_ref[h, c] = scores[1][:, cols]
            e1_ref[h, c] = e1[:, cols]
            e2_ref[h, c] = e2[:, cols]
        tau_ref[h:h + 1, :] = best[PEER_TOPK - 1]


def peer_scores(qp, subkeys, *, tm):
    t = qp.shape[0]
    n_heads, _, n_keys, dk = subkeys.shape
    tm = min(tm, t)
    big = jax.ShapeDtypeStruct((n_heads, t // V7X_LANES, n_keys, V7X_LANES), f32)
    blk = pl.BlockSpec((n_heads, tm // V7X_LANES, n_keys, V7X_LANES), lambda i: (0, i, 0, 0))
    est = 2 * (_nbytes((tm, qp.shape[1]), bf16) + _nbytes(subkeys.shape, bf16) + 4 * _nbytes((n_heads, n_keys, tm), f32)) \
        + 16 * _nbytes((n_keys, tm), f32)
    kern = functools.partial(_peer_score_kernel, n_heads=n_heads, n_keys=n_keys)
    return pl.pallas_call(
        kern,
        out_shape=(big, big, big, big, jax.ShapeDtypeStruct((n_heads, t), f32)),
        grid=(t // tm,),
        in_specs=[pl.BlockSpec((tm, qp.shape[1]), lambda i: (i, 0)),
                  pl.BlockSpec(subkeys.shape, lambda i: (0, 0, 0, 0))],
        out_specs=(blk, blk, blk, blk, pl.BlockSpec((n_heads, tm), lambda i: (0, i))),
        compiler_params=_params(("parallel",), est),
        name="peer_scores",
    )(qp, subkeys)


def _peer_dense_kernel(x_ref, u_ref, v_ref, s1_ref, s2_ref, e1_ref, e2_ref, tau_ref, h_ref, g_ref, o_ref,
                       acc_ref, act_ref, wact_ref, *, n_heads, n_keys):
    e = pl.program_id(1)
    te, tm = act_ref.shape
    rows_per_step = te // n_keys

    @pl.when(e == 0)
    def _():
        acc_ref[...] = jnp.zeros(acc_ref.shape, f32)

    act_ref[...] = lax.dot_general(u_ref[...], x_ref[...], NT_DIMS, preferred_element_type=f32)

    for il in range(rows_per_step):
        i1 = e * rows_per_step + il
        rows = slice(il * n_keys, (il + 1) * n_keys)
        for c in range(tm // V7X_LANES):
            cols = slice(c * V7X_LANES, (c + 1) * V7X_LANES)
            wt = jnp.zeros((n_keys, V7X_LANES), f32)
            for h in range(n_heads):
                pair = s2_ref[h, c] + s1_ref[h, c, pl.ds(i1, 1), :]
                weight = e2_ref[h, c] * e1_ref[h, c, pl.ds(i1, 1), :]
                wt = wt + jnp.where(pair >= tau_ref[h:h + 1, cols], weight, 0.0)
            wact_ref[cols, rows] = (wt * jax.nn.gelu(act_ref[rows, cols])).T.astype(bf16)

    acc_ref[...] += jnp.dot(wact_ref[...], v_ref[...], preferred_element_type=f32)

    @pl.when(e == pl.num_programs(1) - 1)
    def _():
        r = h_ref[...] + acc_ref[...]
        ms = jnp.mean(r * r, axis=-1, keepdims=True)
        o_ref[...] = r * lax.rsqrt(ms + EPS) * g_ref[...]


def peer_dense(xn, u_tab, v_tab, s1, s2, e1, e2, tau, h, g, *, tm, te):
    t, d = xn.shape
    n_exp = u_tab.shape[0]
    n_heads, _, n_keys, _ = s1.shape
    tm = min(tm, t)
    once = dict(pipeline_mode=pl.Buffered(1))
    sblk = pl.BlockSpec((n_heads, tm // V7X_LANES, n_keys, V7X_LANES), lambda i, e: (0, i, 0, 0), **once)
    est = _nbytes((tm, d), bf16) + 4 * _nbytes((te, d), bf16) + 4 * _nbytes((n_heads, n_keys, tm), f32) \
        + 4 * _nbytes((tm, d), f32) + 2 * _nbytes((te, tm), f32) + 8 * _nbytes((te, tm), f32)
    kern = functools.partial(_peer_dense_kernel, n_heads=n_heads, n_keys=n_keys)
    return pl.pallas_call(
        kern,
        out_shape=jax.ShapeDtypeStruct((t, d), f32),
        grid=(t // tm, n_exp // te),
        in_specs=[
            pl.BlockSpec((tm, d), lambda i, e: (i, 0), **once),
            pl.BlockSpec((te, d), lambda i, e: (e, 0)),
            pl.BlockSpec((te, d), lambda i, e: (e, 0)),
            sblk, sblk, sblk, sblk,
            pl.BlockSpec((n_heads, tm), lambda i, e: (0, i)),
            pl.BlockSpec((tm, d), lambda i, e: (i, 0), **once),
            pl.BlockSpec((1, d), lambda i, e: (0, 0)),
        ],
        out_specs=pl.BlockSpec((tm, d), lambda i, e: (i, 0)),
        scratch_shapes=[pltpu.VMEM((tm, d), f32), pltpu.VMEM((te, tm), f32), pltpu.VMEM((tm, te), bf16)],
        compiler_params=_params(("parallel", "arbitrary"), est),
        name="peer_dense",
    )(xn, u_tab, v_tab, s1, s2, e1, e2, tau, h, g.reshape(1, d))


def _layer(x, prefix, attend, conv_tt, conv_dtype, lw, norm_final_g, *, n_seq, seq_len, peer_tm, peer_te):
    (norm_mix_g, w_in, b_gate, conv_w_dw, conv_b_dw, conv_ln_g, conv_ln_b, w_conv_out, w_attn_out, w_out,
     norm_ffn_g, peer_wq, peer_subkeys, peer_u, peer_v) = lw
    d = x.shape[1]
    conv_ch = conv_w_dw.shape[1]
    attn_w = w_attn_out.shape[0]
    q_col0 = 2 * conv_ch
    gate_col0 = q_col0 + 3 * attn_w

    xn = rmsnorm(x, norm_mix_g)
    proj = matmul(xn, w_in)
    hc, conv_state = conformer_conv(proj, prefix, conv_w_dw, conv_b_dw, conv_ln_g, conv_ln_b,
                                    n_seq=n_seq, seq_len=seq_len, tt=conv_tt, out_dtype=conv_dtype)
    attn = attend(proj, q_col0)
    merged = gated_merge(hc.astype(bf16), attn.astype(bf16), w_conv_out, w_attn_out, proj,
                         b_gate.reshape(1, -1), gate_col0)
    h = matmul(merged, w_out, residual=x)
    xn2 = rmsnorm(h, norm_ffn_g)
    qp = matmul(xn2, peer_wq, out_dtype=bf16)
    s1, s2, e1, e2, tau = peer_scores(qp, peer_subkeys, tm=peer_tm)
    y = peer_dense(xn2, peer_u, peer_v, s1, s2, e1, e2, tau, h, norm_final_g, tm=peer_tm, te=peer_te)
    k = proj[:, q_col0 + attn_w:q_col0 + 2 * attn_w]
    v = proj[:, q_col0 + 2 * attn_w:q_col0 + 3 * attn_w]
    return y, k, v, conv_state


def kernel(x_prompt, x_sample, cache_k, cache_v, state_conv, page_table, rel_bias, norm_mix_g, w_in, b_gate,
           conv_w_dw, conv_b_dw, conv_ln_g, conv_ln_b, w_conv_out, w_attn_out, w_out, norm_ffn_g,
           peer_wq, peer_subkeys, peer_u, peer_v, norm_final_g):
    depth = w_in.shape[0]
    assert depth == 1, "the final norm is fused into the last layer's PEER kernel"
    n_p, seq_p, d = x_prompt.shape
    n_s, seq_s, _ = x_sample.shape
    _, n_pool, page, n_heads, head_dim = cache_k.shape
    width = n_heads * head_dim
    past_len = page_table.shape[1] * page
    conv_w = conv_w_dw.shape[1]
    conv_ch = conv_w_dw.shape[2]

    n_near = sum(1 for pg in range(past_len // page) if past_len - (pg * page + page - 1) < MAX_DISTANCE)

    l = 0
    lw = (norm_mix_g[l], w_in[l].astype(bf16), b_gate[l], conv_w_dw[l], conv_b_dw[l], conv_ln_g[l], conv_ln_b[l],
          w_conv_out[l].astype(bf16), w_attn_out[l].astype(bf16), w_out[l].astype(bf16), norm_ffn_g[l],
          peer_wq[l].astype(bf16), peer_subkeys[l].astype(bf16), peer_u[l].astype(bf16), peer_v[l].astype(bf16))

    p_tabs = prompt_bias_tables(rel_bias)
    page_tabs, own_tab = sample_bias_tables(rel_bias, past_len=past_len, page=page, n_q=seq_s, n_near=n_near)

    attend_p = lambda proj, q_col0: moba_prompt(proj, p_tabs, n_seq=n_p, seq_len=seq_p, n_heads=n_heads,
                                                head_dim=head_dim, q_col0=q_col0)
    attend_s = lambda proj, q_col0: moba_sample(proj, cache_k, cache_v, page_table, page_tabs, own_tab, layer=l,
                                                n_seq=n_s, n_q=seq_s, q_col0=q_col0, n_near=n_near)

    zeros_prefix = jnp.zeros((n_p, conv_w - 1, conv_ch), f32)
    yp, kp, vp, cp = _layer(x_prompt.reshape(n_p * seq_p, d), zeros_prefix, attend_p, 256, bf16, lw, norm_final_g,
                            n_seq=n_p, seq_len=seq_p, peer_tm=512, peer_te=512)
    ys, ks, vs, cs = _layer(x_sample.reshape(n_s * seq_s, d), state_conv[l], attend_s, seq_s, f32, lw, norm_final_g,
                            n_seq=n_s, seq_len=seq_s, peer_tm=256, peer_te=512)

    return (yp.reshape(n_p, seq_p, d), ys.reshape(n_s, seq_s, d),
            kp.reshape(1, n_p, seq_p, n_heads, head_dim), vp.reshape(1, n_p, seq_p, n_heads, head_dim),
            ks.reshape(1, n_s, seq_s, n_heads, head_dim), vs.reshape(1, n_s, seq_s, n_heads, head_dim),
            cp[None], cs[None])
```

```python
import functools
import math

import jax
import jax.numpy as jnp
from jax import lax
from jax.experimental import pallas as pl
from jax.experimental.pallas import tpu as pltpu

f32 = jnp.float32
bf16 = jnp.bfloat16

MOBA_BLOCK = 256
MOBA_TOPK = 3
NUM_BUCKETS = 32
MAX_DISTANCE = 128
PEER_TOPK = 16
EPS = 1e-6
NEG = -1e30

V7X_VMEM_BYTES = 64 * 1024 * 1024
V7X_LANES = 128
V7X_SUBLANES = 8
VMEM_CAP_BYTES = V7X_VMEM_BYTES - 8 * 1024 * 1024

NT_DIMS = (((1,), (1,)), ((), ()))
TN_DIMS = (((0,), (0,)), ((), ()))


def _params(semantics, est_bytes):
    limit = int(min(max(est_bytes, 16 * 1024 * 1024), VMEM_CAP_BYTES))
    return pltpu.CompilerParams(dimension_semantics=semantics, vmem_limit_bytes=limit)


def _nbytes(shape, dtype):
    return math.prod(shape) * jnp.dtype(dtype).itemsize


def _rmsnorm_kernel(x_ref, g_ref, o_ref):
    x = x_ref[...]
    ms = jnp.mean(x * x, axis=-1, keepdims=True)
    o_ref[...] = (x * lax.rsqrt(ms + EPS) * g_ref[...]).astype(o_ref.dtype)


def rmsnorm(x, g, out_dtype=bf16):
    t, d = x.shape
    tr = min(512, t)
    est = 2 * (_nbytes((tr, d), f32) + _nbytes((tr, d), out_dtype)) + 4 * _nbytes((tr, d), f32)
    return pl.pallas_call(
        _rmsnorm_kernel,
        out_shape=jax.ShapeDtypeStruct((t, d), out_dtype),
        grid=(t // tr,),
        in_specs=[pl.BlockSpec((tr, d), lambda i: (i, 0)), pl.BlockSpec((1, d), lambda i: (0, 0))],
        out_specs=pl.BlockSpec((tr, d), lambda i: (i, 0)),
        compiler_params=_params(("parallel",), est),
        name="rmsnorm",
    )(x, g.reshape(1, d))


def _mm_kernel(x_ref, w_ref, o_ref):
    o_ref[...] = jnp.dot(x_ref[...], w_ref[...], preferred_element_type=f32).astype(o_ref.dtype)


def _mm_res_kernel(x_ref, w_ref, r_ref, o_ref):
    acc = jnp.dot(x_ref[...], w_ref[...], preferred_element_type=f32)
    o_ref[...] = (r_ref[...] + acc).astype(o_ref.dtype)


def matmul(x, w, *, col0=0, n=None, out_dtype=f32, residual=None, tm=1024, tn=1024):
    m, k = x.shape
    n = w.shape[1] - col0 if n is None else n
    tm, tn = min(tm, m), min(tn, n)
    assert col0 % tn == 0 and n % tn == 0 and m % tm == 0
    jb = col0 // tn
    est = 2 * (_nbytes((tm, k), bf16) + _nbytes((k, tn), bf16) + _nbytes((tm, tn), out_dtype)
               + _nbytes((tm, tn), f32)) + _nbytes((tm, tn), f32)
    in_specs = [pl.BlockSpec((tm, k), lambda i, j: (i, 0)), pl.BlockSpec((k, tn), lambda i, j: (0, jb + j))]
    args = [x, w]
    body = _mm_kernel
    if residual is not None:
        in_specs.append(pl.BlockSpec((tm, tn), lambda i, j: (i, j)))
        args.append(residual)
        body = _mm_res_kernel
    return pl.pallas_call(
        body,
        out_shape=jax.ShapeDtypeStruct((m, n), out_dtype),
        grid=(m // tm, n // tn),
        in_specs=in_specs,
        out_specs=pl.BlockSpec((tm, tn), lambda i, j: (i, j)),
        compiler_params=_params(("parallel", "parallel"), est),
        name="matmul",
    )(*args)


def _merge_kernel(hc_ref, at_ref, wc_ref, wa_ref, ga_ref, gb_ref, ba_ref, bb_ref, o_ref):
    conv_out = jnp.dot(hc_ref[...], wc_ref[...], preferred_element_type=f32)
    attn_out = jnp.dot(at_ref[...], wa_ref[...], preferred_element_type=f32)
    gate_a = jax.nn.sigmoid(ga_ref[...] + ba_ref[...])
    gate_b = jax.nn.sigmoid(gb_ref[...] + bb_ref[...])
    o_ref[...] = (gate_a * conv_out + gate_b * attn_out).astype(o_ref.dtype)


def gated_merge(hc, attn, w_conv_out, w_attn_out, proj, b_gate, gate_col0, *, tm=512, tn=512):
    m, k = hc.shape
    n = w_conv_out.shape[1]
    tm, tn = min(tm, m), min(tn, n)
    ga0, gb0 = gate_col0 // tn, (gate_col0 + n) // tn
    nb = n // tn
    est = 2 * (2 * _nbytes((tm, k), bf16) + 2 * _nbytes((k, tn), bf16) + 3 * _nbytes((tm, tn), f32)) \
        + 4 * _nbytes((tm, tn), f32)
    return pl.pallas_call(
        _merge_kernel,
        out_shape=jax.ShapeDtypeStruct((m, n), bf16),
        grid=(m // tm, nb),
        in_specs=[
            pl.BlockSpec((tm, k), lambda i, j: (i, 0)),
            pl.BlockSpec((tm, k), lambda i, j: (i, 0)),
            pl.BlockSpec((k, tn), lambda i, j: (0, j)),
            pl.BlockSpec((k, tn), lambda i, j: (0, j)),
            pl.BlockSpec((tm, tn), lambda i, j: (i, ga0 + j)),
            pl.BlockSpec((tm, tn), lambda i, j: (i, gb0 + j)),
            pl.BlockSpec((1, tn), lambda i, j: (0, j)),
            pl.BlockSpec((1, tn), lambda i, j: (0, nb + j)),
        ],
        out_specs=pl.BlockSpec((tm, tn), lambda i, j: (i, j)),
        compiler_params=_params(("parallel", "parallel"), est),
        name="gated_merge",
    )(hc, attn, w_conv_out, w_attn_out, proj, proj, b_gate, b_gate)


CONV_ROW_CHUNK = 64
CONV_LANE_CHUNK = 256
CONV_HALO = 32


def _conv_kernel(a_ref, gt_ref, pre_ref, wdw_ref, bdw_ref, lng_ref, lnb_ref, hc_ref, st_ref, ubuf, ybuf,
                 *, tt, width):
    t = pl.program_id(1)
    c = ubuf.shape[1]
    lead = CONV_HALO - (width - 1)
    rc = min(CONV_ROW_CHUNK, tt)

    @pl.when(t == 0)
    def _():
        ubuf[0:lead, :] = jnp.zeros((lead, c), f32)
        ubuf[lead:CONV_HALO, :] = pre_ref[...]

    ubuf[CONV_HALO:CONV_HALO + tt, :] = a_ref[...] * jax.nn.sigmoid(gt_ref[...])

    n_lane = c // CONV_LANE_CHUNK

    def chunk(idx, carry):
        r0 = pl.multiple_of((idx // n_lane) * rc, rc)
        c0 = pl.multiple_of((idx % n_lane) * CONV_LANE_CHUNK, CONV_LANE_CHUNK)
        win = ubuf[pl.ds(r0, rc + CONV_HALO), pl.ds(c0, CONV_LANE_CHUNK)]
        acc = jnp.broadcast_to(bdw_ref[:, pl.ds(c0, CONV_LANE_CHUNK)], (rc, CONV_LANE_CHUNK))
        for phase in range(V7X_SUBLANES):
            offs = [o for o in range(lead, lead + width) if o % V7X_SUBLANES == phase]
            if not offs:
                continue
            shifted = win if phase == 0 else pltpu.roll(win, rc + CONV_HALO - phase, 0)
            for o in offs:
                q = o - phase
                w_row = wdw_ref[o - lead:o - lead + 1, pl.ds(c0, CONV_LANE_CHUNK)]
                acc = acc + shifted[q:q + rc] * w_row
        ybuf[pl.ds(r0, rc), pl.ds(c0, CONV_LANE_CHUNK)] = acc
        return carry

    lax.fori_loop(0, (tt // rc) * n_lane, chunk, 0)

    def ln_chunk(i, carry):
        r0 = pl.multiple_of(i * rc, rc)
        y = ybuf[pl.ds(r0, rc), :]
        mu = jnp.mean(y, axis=-1, keepdims=True)
        yc = y - mu
        var = jnp.mean(yc * yc, axis=-1, keepdims=True)
        yn = yc * lax.rsqrt(var + EPS) * lng_ref[...] + lnb_ref[...]
        hc_ref[pl.ds(r0, rc), :] = (yn * jax.nn.sigmoid(yn)).astype(hc_ref.dtype)
        return carry

    lax.fori_loop(0, tt // rc, ln_chunk, 0)

    @pl.when(t == pl.num_programs(1) - 1)
    def _():
        st_ref[...] = ubuf[tt + lead:tt + CONV_HALO, :]

    ubuf[0:CONV_HALO, :] = ubuf[tt:tt + CONV_HALO, :]


def conformer_conv(proj, prefix, w_dw, b_dw, ln_g, ln_b, *, n_seq, seq_len, tt, out_dtype):
    width, c = w_dw.shape
    nt = seq_len // tt
    est = 2 * (2 * _nbytes((tt, c), f32) + _nbytes((width - 1, c), f32) + _nbytes((width, c), f32)
               + _nbytes((tt, c), out_dtype) + _nbytes((width - 1, c), f32)) \
        + _nbytes((2 * tt + CONV_HALO, c), f32) + 8 * _nbytes((CONV_ROW_CHUNK, c), f32)
    kern = functools.partial(_conv_kernel, tt=tt, width=width)
    return pl.pallas_call(
        kern,
        out_shape=(jax.ShapeDtypeStruct((n_seq * seq_len, c), out_dtype),
                   jax.ShapeDtypeStruct((n_seq, width - 1, c), f32)),
        grid=(n_seq, nt),
        in_specs=[
            pl.BlockSpec((tt, c), lambda b, t: (b * nt + t, 0)),
            pl.BlockSpec((tt, c), lambda b, t: (b * nt + t, 1)),
            pl.BlockSpec((None, width - 1, c), lambda b, t: (b, 0, 0)),
            pl.BlockSpec((width, c), lambda b, t: (0, 0)),
            pl.BlockSpec((1, c), lambda b, t: (0, 0)),
            pl.BlockSpec((1, c), lambda b, t: (0, 0)),
            pl.BlockSpec((1, c), lambda b, t: (0, 0)),
        ],
        out_specs=(pl.BlockSpec((tt, c), lambda b, t: (b * nt + t, 0)),
                   pl.BlockSpec((None, width - 1, c), lambda b, t: (b, 0, 0))),
        scratch_shapes=[pltpu.VMEM((tt + CONV_HALO, c), f32), pltpu.VMEM((tt, c), f32)],
        compiler_params=_params(("parallel", "arbitrary"), est),
        name="conformer_conv",
    )(proj, proj, prefix, w_dw, b_dw.reshape(1, c), ln_g.reshape(1, c), ln_b.reshape(1, c))


def _rel_bucket(dist):
    n = jnp.maximum(dist, 0)
    max_exact = NUM_BUCKETS // 2
    large = max_exact + (jnp.log(jnp.maximum(n, 1).astype(f32) / max_exact)
                         / math.log(MAX_DISTANCE / max_exact) * (NUM_BUCKETS - max_exact)).astype(jnp.int32)
    large = jnp.minimum(large, NUM_BUCKETS - 1)
    return jnp.where(n < max_exact, n, large)


def _bias_from_dist(dist, lookup):
    bucket = _rel_bucket(dist)
    out = jnp.zeros(dist.shape, f32)
    for b in range(NUM_BUCKETS):
        out = jnp.where(bucket == b, lookup(b), out)
    return jnp.where(dist >= 0, out, NEG)


def _prompt_bias_kernel(rb_ref, o_ref):
    h = pl.program_id(0)
    ki = lax.broadcasted_iota(jnp.int32, (MOBA_BLOCK, MOBA_BLOCK), 0)
    qi = lax.broadcasted_iota(jnp.int32, (MOBA_BLOCK, MOBA_BLOCK), 1)
    for tab in range(3):
        dist = tab * MOBA_BLOCK + qi - ki
        o_ref[tab] = _bias_from_dist(dist, lambda b: rb_ref[b, h])


def prompt_bias_tables(rel_bias):
    n_heads = rel_bias.shape[1]
    return pl.pallas_call(
        _prompt_bias_kernel,
        out_shape=jax.ShapeDtypeStruct((n_heads, 3, MOBA_BLOCK, MOBA_BLOCK), f32),
        grid=(n_heads,),
        in_specs=[pl.BlockSpec(memory_space=pltpu.SMEM)],
        out_specs=pl.BlockSpec((None, 3, MOBA_BLOCK, MOBA_BLOCK), lambda h: (h, 0, 0, 0)),
        compiler_params=_params(("parallel",), 16 * 1024 * 1024),
        name="prompt_bias_tables",
    )(rel_bias)


def _sample_bias_kernel(rb_ref, page_ref, own_ref, *, past_len, page, n_q, n_near, n_heads):
    n_rows = n_heads * n_q
    rowh = lax.broadcasted_iota(jnp.int32, (n_rows, 1), 0) // n_q

    def lookup(b):
        out = jnp.zeros((n_rows, 1), f32)
        for hh in range(n_heads):
            out = jnp.where(rowh == hh, rb_ref[b, hh], out)
        return out

    def table(width, col_head, dist):
        row = lax.broadcasted_iota(jnp.int32, (n_rows, width), 0)
        col = lax.broadcasted_iota(jnp.int32, (n_rows, width), 1)
        same_head = (row // n_q) == col_head(col)
        return jnp.where(same_head, _bias_from_dist(dist(row % n_q, col), lookup), NEG)

    n_pages = past_len // page
    wide = page * n_heads
    page_ref[0] = table(wide, lambda c: c % n_heads, lambda q, c: jnp.full(c.shape, MAX_DISTANCE, jnp.int32))
    for i in range(n_near):
        pg = n_pages - n_near + i
        page_ref[1 + i] = table(wide, lambda c: c % n_heads,
                                lambda q, c: past_len + q - (pg * page + c // n_heads))
    own_ref[...] = table(n_rows, lambda c: c // n_q, lambda q, c: q - c % n_q)


def sample_bias_tables(rel_bias, *, past_len, page, n_q, n_near):
    n_heads = rel_bias.shape[1]
    n_rows = n_heads * n_q
    kern = functools.partial(_sample_bias_kernel, past_len=past_len, page=page, n_q=n_q, n_near=n_near,
                             n_heads=n_heads)
    return pl.pallas_call(
        kern,
        out_shape=(jax.ShapeDtypeStruct((n_near + 1, n_rows, page * n_heads), f32),
                   jax.ShapeDtypeStruct((n_rows, n_rows), f32)),
        in_specs=[pl.BlockSpec(memory_space=pltpu.SMEM)],
        out_specs=(pl.BlockSpec(memory_space=pltpu.VMEM), pl.BlockSpec(memory_space=pltpu.VMEM)),
        name="sample_bias_tables",
    )(rel_bias)


MOBA_HEAD_GROUP = 4


def _moba_prompt_kernel(q_ref, k_ref, v_ref, bias_ref, o_ref,
                        kb_ref, vt_ref, mean_ref, qb_ref, sel_ref, m_ref, l_ref, acc_ref,
                        s_ref, p_ref, alpha_ref, *, head_dim):
    c = pl.program_id(2)
    n_blocks = k_ref.shape[0] // MOBA_BLOCK
    scale = head_dim ** -0.5

    for hh in range(MOBA_HEAD_GROUP):
        lanes = slice(hh * head_dim, (hh + 1) * head_dim)

        @pl.when(c == 0)
        def _():
            kf = k_ref[:, lanes]
            kb_ref[hh] = kf.astype(bf16)
            vt_ref[hh] = v_ref[:, lanes].T.astype(bf16)
            mean_ref[hh] = jnp.mean(kf.reshape(n_blocks, MOBA_BLOCK, head_dim), axis=1)

        qf = q_ref[:, lanes]
        st = lax.dot_general(mean_ref[hh], qf, NT_DIMS, preferred_element_type=f32,
                             precision=lax.Precision.HIGHEST)
        row = lax.broadcasted_iota(jnp.int32, st.shape, 0)
        sm = jnp.where(row < c, st, NEG)
        for j in range(n_blocks):
            rj = sm[j:j + 1, :]
            beats = jnp.where((sm > rj) | ((sm == rj) & (row < j)), 1.0, 0.0)
            rank = jnp.sum(beats, axis=0, keepdims=True)
            jv = jnp.full(rank.shape, j, jnp.int32)
            chosen = jnp.where((rank < MOBA_TOPK) & (jv < c), 1.0, 0.0)
            sel_ref[hh, j] = jnp.where(jv == c, 1.0, chosen)

        qb_ref[hh] = qf.astype(bf16)
        m_ref[hh] = jnp.full(m_ref.shape[1:], NEG, f32)
        l_ref[hh] = jnp.zeros(l_ref.shape[1:], f32)
        acc_ref[hh] = jnp.zeros(acc_ref.shape[1:], f32)

    def tile(i, carry):
        j = c - i
        tab = jnp.minimum(i, 2)
        k0 = pl.multiple_of(j * MOBA_BLOCK, MOBA_BLOCK)
        for hh in range(MOBA_HEAD_GROUP):
            kj = kb_ref[hh, pl.ds(k0, MOBA_BLOCK), :]
            s_ref[hh] = lax.dot_general(kj, qb_ref[hh], NT_DIMS, preferred_element_type=f32)
        for hh in range(MOBA_HEAD_GROUP):
            lg = s_ref[hh] * scale + bias_ref[hh, tab]
            lg = jnp.where(sel_ref[hh, j] > 0.5, lg, NEG)
            m_old = m_ref[hh]
            m_new = jnp.maximum(m_old, jnp.max(lg, axis=0, keepdims=True))
            p = jnp.exp(lg - m_new)
            alpha = jnp.exp(m_old - m_new)
            l_ref[hh] = alpha * l_ref[hh] + jnp.sum(p, axis=0, keepdims=True)
            p_ref[hh] = p.astype(bf16)
            alpha_ref[hh] = alpha
            m_ref[hh] = m_new
        for hh in range(MOBA_HEAD_GROUP):
            vtj = vt_ref[hh, :, pl.ds(k0, MOBA_BLOCK)]
            acc_ref[hh] = alpha_ref[hh] * acc_ref[hh] + jnp.dot(vtj, p_ref[hh], preferred_element_type=f32)
        return carry

    lax.fori_loop(0, c + 1, tile, 0)
    for hh in range(MOBA_HEAD_GROUP):
        o_ref[:, hh * head_dim:(hh + 1) * head_dim] = (acc_ref[hh] / l_ref[hh]).T.astype(o_ref.dtype)


def moba_prompt(proj, k, v, bias_tabs, *, n_seq, seq_len, n_heads, head_dim, q_col0):
    width = n_heads * head_dim
    gw = MOBA_HEAD_GROUP * head_dim
    n_groups = n_heads // MOBA_HEAD_GROUP
    n_blocks = seq_len // MOBA_BLOCK
    qb0 = q_col0 // gw
    est = 2 * (2 * _nbytes((MOBA_BLOCK, gw), f32) + 2 * _nbytes((seq_len, gw), f32)
               + _nbytes((MOBA_HEAD_GROUP, 3, MOBA_BLOCK, MOBA_BLOCK), f32)) \
        + 2 * _nbytes((MOBA_HEAD_GROUP, seq_len, head_dim), bf16) + 8 * _nbytes((MOBA_BLOCK, MOBA_BLOCK), f32)
    kern = functools.partial(_moba_prompt_kernel, head_dim=head_dim)
    return pl.pallas_call(
        kern,
        out_shape=jax.ShapeDtypeStruct((n_seq * seq_len, width), bf16),
        grid=(n_seq, n_groups, n_blocks),
        in_specs=[
            pl.BlockSpec((MOBA_BLOCK, gw), lambda b, g, c: (b * n_blocks + c, qb0 + g)),
            pl.BlockSpec((seq_len, gw), lambda b, g, c: (b, g)),
            pl.BlockSpec((seq_len, gw), lambda b, g, c: (b, g)),
            pl.BlockSpec((MOBA_HEAD_GROUP, 3, MOBA_BLOCK, MOBA_BLOCK), lambda b, g, c: (g, 0, 0, 0)),
        ],
        out_specs=pl.BlockSpec((MOBA_BLOCK, gw), lambda b, g, c: (b * n_blocks + c, g)),
        scratch_shapes=[
            pltpu.VMEM((MOBA_HEAD_GROUP, seq_len, head_dim), bf16),
            pltpu.VMEM((MOBA_HEAD_GROUP, head_dim, seq_len), bf16),
            pltpu.VMEM((MOBA_HEAD_GROUP, n_blocks, head_dim), f32),
            pltpu.VMEM((MOBA_HEAD_GROUP, MOBA_BLOCK, head_dim), bf16),
            pltpu.VMEM((MOBA_HEAD_GROUP, n_blocks, 1, MOBA_BLOCK), f32),
            pltpu.VMEM((MOBA_HEAD_GROUP, 1, MOBA_BLOCK), f32),
            pltpu.VMEM((MOBA_HEAD_GROUP, 1, MOBA_BLOCK), f32),
            pltpu.VMEM((MOBA_HEAD_GROUP, head_dim, MOBA_BLOCK), f32),
            pltpu.VMEM((MOBA_HEAD_GROUP, MOBA_BLOCK, MOBA_BLOCK), f32),
            pltpu.VMEM((MOBA_HEAD_GROUP, MOBA_BLOCK, MOBA_BLOCK), bf16),
            pltpu.VMEM((MOBA_HEAD_GROUP, 1, MOBA_BLOCK), f32),
        ],
        compiler_params=_params(("parallel", "parallel", "arbitrary"), est),
        name="moba_prompt",
    )(proj, k, v, bias_tabs)


def _moba_sample_kernel(pt_ref, q_ref, kn_ref, vn_ref, *rest,
                        n_heads, head_dim, n_q, n_near, n_pages, pages_per_block):
    del pt_ref
    kc_refs = rest[:pages_per_block]
    vc_refs = rest[pages_per_block:2 * pages_per_block]
    tab_ref, own_ref, o_ref, qb_ref, qf_ref, own_kv_ref, score_ref, m_ref, l_ref, part_ref = rest[2 * pages_per_block:]
    s = pl.program_id(1)
    n_blocks = n_pages // pages_per_block
    page = kc_refs[0].shape[0]
    n_rows = n_heads * n_q
    scale = head_dim ** -0.5
    lane = lax.broadcasted_iota(jnp.int32, (n_rows, V7X_LANES), 1)

    @pl.when(s == 0)
    def _():
        for h in range(n_heads):
            qf_ref[h * n_q:(h + 1) * n_q, :] = q_ref[:, h * head_dim:(h + 1) * head_dim]
        qb_ref[...] = qf_ref[...].astype(bf16)
        score_ref[...] = jnp.full(score_ref.shape, -jnp.inf, f32)
        m_ref[...] = jnp.full(m_ref.shape, NEG, f32)
        l_ref[...] = jnp.zeros(l_ref.shape, f32)

    def record(ref, slot, col):
        ref[...] = jnp.where(lane == slot, col, ref[...])

    def partial_softmax(slots, logits, values):
        ps = []
        for slot, lg in zip(slots, logits):
            m = jnp.max(lg, axis=1, keepdims=True)
            p = jnp.exp(lg - m)
            record(m_ref, slot, m)
            record(l_ref, slot, jnp.sum(p, axis=1, keepdims=True))
            ps.append(p.astype(bf16))
        for slot, p, v in zip(slots, ps, values):
            part_ref[slot] = jnp.dot(p, v, preferred_element_type=f32)

    @pl.when(s < n_blocks)
    def _():
        slots, logits, values = [], [], []
        ksum = jnp.zeros((n_heads, head_dim), f32)
        for i in range(pages_per_block):
            pg = s * pages_per_block + i
            k3 = kc_refs[i][...]
            ksum = ksum + jnp.sum(k3, axis=0)
            k2 = k3.reshape(page * n_heads, head_dim).astype(bf16)
            raw = lax.dot_general(qb_ref[...], k2, NT_DIMS, preferred_element_type=f32)
            near = pg - (n_pages - n_near)
            slots.append(pg)
            logits.append(raw * scale + tab_ref[jnp.maximum(near + 1, 0)])
            values.append(vc_refs[i][...].reshape(page * n_heads, head_dim).astype(bf16))
        ksum_rows = jnp.concatenate(
            [jnp.broadcast_to(ksum[h:h + 1, :], (n_q, head_dim)) for h in range(n_heads)], axis=0)
        record(score_ref, s, jnp.sum(qf_ref[...] * ksum_rows, axis=1, keepdims=True))
        partial_softmax(slots, logits, values)

    @pl.when(s == n_blocks)
    def _():
        for h in range(n_heads):
            own_kv_ref[0, h * n_q:(h + 1) * n_q, :] = kn_ref[:, h * head_dim:(h + 1) * head_dim]
            own_kv_ref[1, h * n_q:(h + 1) * n_q, :] = vn_ref[:, h * head_dim:(h + 1) * head_dim]
        raw = lax.dot_general(qb_ref[...], own_kv_ref[0].astype(bf16), NT_DIMS, preferred_element_type=f32)
        partial_softmax([n_pages], [raw * scale + own_ref[...]], [own_kv_ref[1].astype(bf16)])

        score = score_ref[...]
        sel = jnp.where(lane == n_pages, 1.0, 0.0)
        for _ in range(MOBA_TOPK):
            mx = jnp.max(score, axis=1, keepdims=True)
            first = jnp.min(jnp.where(score == mx, lane, V7X_LANES), axis=1, keepdims=True)
            sel = jnp.where((lane // pages_per_block == first) & (lane < n_pages), 1.0, sel)
            score = jnp.where(lane == first, -jnp.inf, score)

        m_all, l_all = m_ref[...], l_ref[...]
        m_top = jnp.max(jnp.where(sel > 0.5, m_all, NEG), axis=1, keepdims=True)
        w = jnp.where(sel > 0.5, jnp.exp(m_all - m_top), 0.0)
        wn = w / jnp.sum(w * l_all, axis=1, keepdims=True)
        out = jnp.zeros((n_rows, head_dim), f32)
        for i in range(n_pages + 1):
            out = out + wn[:, i:i + 1] * part_ref[i]
        for h in range(n_heads):
            o_ref[:, h * head_dim:(h + 1) * head_dim] = out[h * n_q:(h + 1) * n_q, :].astype(o_ref.dtype)


def moba_sample(proj, k_new, v_new, cache_k, cache_v, page_table, page_tabs, own_tab, *,
                layer, n_seq, n_q, q_col0, n_near):
    _, _, page, n_heads, head_dim = cache_k.shape
    width = n_heads * head_dim
    n_rows = n_heads * n_q
    n_pages = page_table.shape[1]
    pages_per_block = MOBA_BLOCK // page
    n_blocks = n_pages // pages_per_block
    qb0 = q_col0 // width
    assert n_rows == V7X_LANES and MOBA_TOPK <= n_blocks and n_pages < V7X_LANES and n_pages % pages_per_block == 0
    page_bytes = _nbytes((page, n_heads, head_dim), f32)
    est = 2 * (4 * _nbytes((n_q, width), f32) + 2 * pages_per_block * page_bytes
               + _nbytes(page_tabs.shape, f32) + _nbytes(own_tab.shape, f32)) \
        + _nbytes((n_pages + 1, n_rows, head_dim), f32) + 8 * pages_per_block * page_bytes
    kern = functools.partial(_moba_sample_kernel, n_heads=n_heads, head_dim=head_dim, n_q=n_q,
                             n_near=n_near, n_pages=n_pages, pages_per_block=pages_per_block)

    def page_spec(i):
        def index(n, s, pt):
            return (layer, pt[n, jnp.minimum(s * pages_per_block + i, n_pages - 1)], 0, 0, 0)
        return pl.BlockSpec((None, None, page, n_heads, head_dim), index)

    pages = [page_spec(i) for i in range(pages_per_block)]
    stat = pltpu.VMEM((n_rows, V7X_LANES), f32)
    grid_spec = pltpu.PrefetchScalarGridSpec(
        num_scalar_prefetch=1,
        grid=(n_seq, n_blocks + 1),
        in_specs=[
            pl.BlockSpec((n_q, width), lambda n, s, pt: (n, qb0)),
            pl.BlockSpec((n_q, width), lambda n, s, pt: (n, 0)),
            pl.BlockSpec((n_q, width), lambda n, s, pt: (n, 0)),
            *pages, *pages,
            pl.BlockSpec(page_tabs.shape, lambda n, s, pt: (0, 0, 0)),
            pl.BlockSpec(own_tab.shape, lambda n, s, pt: (0, 0)),
        ],
        out_specs=pl.BlockSpec((n_q, width), lambda n, s, pt: (n, 0)),
        scratch_shapes=[
            pltpu.VMEM((n_rows, head_dim), bf16),
            pltpu.VMEM((n_rows, head_dim), f32),
            pltpu.VMEM((2, n_rows, head_dim), f32),
            stat, stat, stat,
            pltpu.VMEM((n_pages + 1, n_rows, head_dim), f32),
        ],
    )
    return pl.pallas_call(
        kern,
        out_shape=jax.ShapeDtypeStruct((n_seq * n_q, width), f32),
        grid_spec=grid_spec,
        compiler_params=_params(("parallel", "arbitrary"), est),
        name="moba_sample",
    )(page_table, proj, k_new, v_new, *([cache_k] * pages_per_block), *([cache_v] * pages_per_block),
      page_tabs, own_tab)


def _peer_score_kernel(q_ref, sk_ref, s1_ref, s2_ref, e1_ref, e2_ref, tau_ref, *, n_heads, n_keys):
    dk = sk_ref.shape[3]
    neg_inf = -jnp.inf
    pairs = [(i, j) for i in range(PEER_TOPK) for j in range(PEER_TOPK) if (i + 1) * (j + 1) <= PEER_TOPK]
    n_cand = -(-len(pairs) // V7X_SUBLANES) * V7X_SUBLANES
    for h in range(n_heads):
        scores, tops = [], []
        for part in range(2):
            c0 = (2 * h + part) * dk
            s = lax.dot_general(sk_ref[h, part], q_ref[:, c0:c0 + dk], NT_DIMS, preferred_element_type=f32)
            scores.append(s)
            vals, w = [], s
            for _ in range(PEER_TOPK):
                mx = jnp.max(w, axis=0, keepdims=True)
                vals.append(mx)
                w = jnp.where(w == mx, neg_inf, w)
            tops.append(vals)
        a, b = tops
        tm = a[0].shape[1]
        cand = jnp.concatenate([a[i] + b[j] for i, j in pairs]
                               + [jnp.full((n_cand - len(pairs), tm), neg_inf, f32)], axis=0)
        rowi = lax.broadcasted_iota(jnp.int32, cand.shape, 0)
        best = []
        for _ in range(PEER_TOPK):
            mx = jnp.max(cand, axis=0, keepdims=True)
            first = jnp.min(jnp.where(cand == mx, rowi, n_cand), axis=0, keepdims=True)
            best.append(mx)
            cand = jnp.where(rowi == first, neg_inf, cand)
        z = jnp.zeros_like(best[0])
        for c in best:
            z = z + jnp.exp(c - best[0])
        e1 = jnp.exp(scores[0] - a[0])
        e2 = jnp.exp(scores[1] - b[0]) / z
        for c in range(tm // V7X_LANES):
            cols = slice(c * V7X_LANES, (c + 1) * V7X_LANES)
            s1_ref[h, c] = scores[0][:, cols]
            s2_ref[h, c] = scores[1][:, cols]
            e1_ref[h, c] = e1[:, cols]
            e2_ref[h, c] = e2[:, cols]
        tau_ref[h:h + 1, :] = best[PEER_TOPK - 1]


def peer_scores(qp, subkeys, *, tm):
    t = qp.shape[0]
    n_heads, _, n_keys, dk = subkeys.shape
    tm = min(tm, t)
    big = jax.ShapeDtypeStruct((n_heads, t // V7X_LANES, n_keys, V7X_LANES), f32)
    blk = pl.BlockSpec((n_heads, tm // V7X_LANES, n_keys, V7X_LANES), lambda i: (0, i, 0, 0))
    est = 2 * (_nbytes((tm, qp.shape[1]), bf16) + _nbytes(subkeys.shape, bf16) + 4 * _nbytes((n_heads, n_keys, tm), f32)) \
        + 16 * _nbytes((n_keys, tm), f32)
    kern = functools.partial(_peer_score_kernel, n_heads=n_heads, n_keys=n_keys)
    return pl.pallas_call(
        kern,
        out_shape=(big, big, big, big, jax.ShapeDtypeStruct((n_heads, t), f32)),
        grid=(t // tm,),
        in_specs=[pl.BlockSpec((tm, qp.shape[1]), lambda i: (i, 0)),
                  pl.BlockSpec(subkeys.shape, lambda i: (0, 0, 0, 0))],
        out_specs=(blk, blk, blk, blk, pl.BlockSpec((n_heads, tm), lambda i: (0, i))),
        compiler_params=_params(("parallel",), est),
        name="peer_scores",
    )(qp, subkeys)


def _peer_dense_kernel(x_ref, u_ref, v_ref, s1_ref, s2_ref, e1_ref, e2_ref, tau_ref, h_ref, g_ref, o_ref,
                       acc_ref, act_ref, wact_ref, *, n_heads, n_keys):
    e = pl.program_id(1)
    te, tm = act_ref.shape
    rows_per_step = te // n_keys

    @pl.when(e == 0)
    def _():
        acc_ref[...] = jnp.zeros(acc_ref.shape, f32)

    act_ref[...] = lax.dot_general(u_ref[...], x_ref[...], NT_DIMS, preferred_element_type=f32)

    for il in range(rows_per_step):
        i1 = e * rows_per_step + il
        rows = slice(il * n_keys, (il + 1) * n_keys)
        for c in range(tm // V7X_LANES):
            cols = slice(c * V7X_LANES, (c + 1) * V7X_LANES)
            wt = jnp.zeros((n_keys, V7X_LANES), f32)
            for h in range(n_heads):
                pair = s2_ref[h, c] + s1_ref[h, c, pl.ds(i1, 1), :]
                weight = e2_ref[h, c] * e1_ref[h, c, pl.ds(i1, 1), :]
                wt = wt + jnp.where(pair >= tau_ref[h:h + 1, cols], weight, 0.0)
            wact_ref[cols, rows] = (wt * jax.nn.gelu(act_ref[rows, cols])).T.astype(bf16)

    acc_ref[...] += jnp.dot(wact_ref[...], v_ref[...], preferred_element_type=f32)

    @pl.when(e == pl.num_programs(1) - 1)
    def _():
        r = h_ref[...] + acc_ref[...]
        ms = jnp.mean(r * r, axis=-1, keepdims=True)
        o_ref[...] = r * lax.rsqrt(ms + EPS) * g_ref[...]


def peer_dense(xn, u_tab, v_tab, s1, s2, e1, e2, tau, h, g, *, tm, te):
    t, d = xn.shape
    n_exp = u_tab.shape[0]
    n_heads, _, n_keys, _ = s1.shape
    tm = min(tm, t)
    once = dict(pipeline_mode=pl.Buffered(1))
    sblk = pl.BlockSpec((n_heads, tm // V7X_LANES, n_keys, V7X_LANES), lambda i, e: (0, i, 0, 0), **once)
    est = _nbytes((tm, d), bf16) + 4 * _nbytes((te, d), bf16) + 4 * _nbytes((n_heads, n_keys, tm), f32) \
        + 4 * _nbytes((tm, d), f32) + 2 * _nbytes((te, tm), f32) + 8 * _nbytes((te, tm), f32)
    kern = functools.partial(_peer_dense_kernel, n_heads=n_heads, n_keys=n_keys)
    return pl.pallas_call(
        kern,
        out_shape=jax.ShapeDtypeStruct((t, d), f32),
        grid=(t // tm, n_exp // te),
        in_specs=[
            pl.BlockSpec((tm, d), lambda i, e: (i, 0), **once),
            pl.BlockSpec((te, d), lambda i, e: (e, 0)),
            pl.BlockSpec((te, d), lambda i, e: (e, 0)),
            sblk, sblk, sblk, sblk,
            pl.BlockSpec((n_heads, tm), lambda i, e: (0, i)),
            pl.BlockSpec((tm, d), lambda i, e: (i, 0), **once),
            pl.BlockSpec((1, d), lambda i, e: (0, 0)),
        ],
        out_specs=pl.BlockSpec((tm, d), lambda i, e: (i, 0)),
        scratch_shapes=[pltpu.VMEM((tm, d), f32), pltpu.VMEM((te, tm), f32), pltpu.VMEM((tm, te), bf16)],
        compiler_params=_params(("parallel", "arbitrary"), est),
        name="peer_dense",
    )(xn, u_tab, v_tab, s1, s2, e1, e2, tau, h, g.reshape(1, d))


def _layer(x, prefix, attend, conv_tt, conv_dtype, lw, norm_final_g, *, n_seq, seq_len, peer_tm, peer_te):
    (norm_mix_g, w_in, b_gate, conv_w_dw, conv_b_dw, conv_ln_g, conv_ln_b, w_conv_out, w_attn_out, w_out,
     norm_ffn_g, peer_wq, peer_subkeys, peer_u, peer_v) = lw
    d = x.shape[1]
    conv_ch = conv_w_dw.shape[1]
    attn_w = w_attn_out.shape[0]
    q_col0 = 2 * conv_ch
    gate_col0 = q_col0 + 3 * attn_w

    xn = rmsnorm(x, norm_mix_g)
    proj = matmul(xn, w_in, col0=0, n=q_col0 + attn_w)
    k = matmul(xn, w_in, col0=q_col0 + attn_w, n=attn_w)
    v = matmul(xn, w_in, col0=q_col0 + 2 * attn_w, n=attn_w)
    gates = matmul(xn, w_in, col0=gate_col0)
    hc, conv_state = conformer_conv(proj, prefix, conv_w_dw, conv_b_dw, conv_ln_g, conv_ln_b,
                                    n_seq=n_seq, seq_len=seq_len, tt=conv_tt, out_dtype=conv_dtype)
    attn = attend(proj, k, v, q_col0)
    merged = gated_merge(hc.astype(bf16), attn.astype(bf16), w_conv_out, w_attn_out, gates,
                         b_gate.reshape(1, -1), 0)
    h = matmul(merged, w_out, residual=x)
    xn2 = rmsnorm(h, norm_ffn_g)
    qp = matmul(xn2, peer_wq, out_dtype=bf16)
    s1, s2, e1, e2, tau = peer_scores(qp, peer_subkeys, tm=peer_tm)
    y = peer_dense(xn2, peer_u, peer_v, s1, s2, e1, e2, tau, h, norm_final_g, tm=peer_tm, te=peer_te)
    return y, k, v, conv_state


def kernel(x_prompt, x_sample, cache_k, cache_v, state_conv, page_table, rel_bias, norm_mix_g, w_in, b_gate,
           conv_w_dw, conv_b_dw, conv_ln_g, conv_ln_b, w_conv_out, w_attn_out, w_out, norm_ffn_g,
           peer_wq, peer_subkeys, peer_u, peer_v, norm_final_g):
    depth = w_in.shape[0]
    assert depth == 1, "the final norm is fused into the last layer's PEER kernel"
    n_p, seq_p, d = x_prompt.shape
    n_s, seq_s, _ = x_sample.shape
    _, n_pool, page, n_heads, head_dim = cache_k.shape
    width = n_heads * head_dim
    past_len = page_table.shape[1] * page
    conv_w = conv_w_dw.shape[1]
    conv_ch = conv_w_dw.shape[2]

    n_near = sum(1 for pg in range(past_len // page) if past_len - (pg * page + page - 1) < MAX_DISTANCE)

    l = 0
    lw = (norm_mix_g[l], w_in[l].astype(bf16), b_gate[l], conv_w_dw[l], conv_b_dw[l], conv_ln_g[l], conv_ln_b[l],
          w_conv_out[l].astype(bf16), w_attn_out[l].astype(bf16), w_out[l].astype(bf16), norm_ffn_g[l],
          peer_wq[l].astype(bf16), peer_subkeys[l].astype(bf16), peer_u[l].astype(bf16), peer_v[l].astype(bf16))

    p_tabs = prompt_bias_tables(rel_bias)
    page_tabs, own_tab = sample_bias_tables(rel_bias, past_len=past_len, page=page, n_q=seq_s, n_near=n_near)

    attend_p = lambda proj, k, v, q_col0: moba_prompt(proj, k, v, p_tabs, n_seq=n_p, seq_len=seq_p,
                                                      n_heads=n_heads, head_dim=head_dim, q_col0=q_col0)
    attend_s = lambda proj, k, v, q_col0: moba_sample(proj, k, v, cache_k, cache_v, page_table, page_tabs, own_tab,
                                                      layer=l, n_seq=n_s, n_q=seq_s, q_col0=q_col0, n_near=n_near)

    zeros_prefix = jnp.zeros((n_p, conv_w - 1, conv_ch), f32)
    yp, kp, vp, cp = _layer(x_prompt.reshape(n_p * seq_p, d), zeros_prefix, attend_p, 256, bf16, lw, norm_final_g,
                            n_seq=n_p, seq_len=seq_p, peer_tm=512, peer_te=1024)
    ys, ks, vs, cs = _layer(x_sample.reshape(n_s * seq_s, d), state_conv[l], attend_s, seq_s, f32, lw, norm_final_g,
                            n_seq=n_s, seq_len=seq_s, peer_tm=256, peer_te=512)

    return (yp.reshape(n_p, seq_p, d), ys.reshape(n_s, seq_s, d),
            kp.reshape(1, n_p, seq_p, n_heads, head_dim), vp.reshape(1, n_p, seq_p, n_heads, head_dim),
            ks.reshape(1, n_s, seq_s, n_heads, head_dim), vs.reshape(1, n_s, seq_s, n_heads, head_dim),
            cp[None], cs[None])
```

```python
import functools
import math

import jax
import jax.numpy as jnp
from jax import lax
from jax.experimental import pallas as pl
from jax.experimental.pallas import tpu as pltpu

f32 = jnp.float32
bf16 = jnp.bfloat16

MOBA_BLOCK = 256
MOBA_TOPK = 3
NUM_BUCKETS = 32
MAX_DISTANCE = 128
PEER_TOPK = 16
EPS = 1e-6
NEG = -1e30

V7X_VMEM_BYTES = 64 * 1024 * 1024
V7X_LANES = 128
V7X_SUBLANES = 8
VMEM_CAP_BYTES = V7X_VMEM_BYTES - 8 * 1024 * 1024

NT_DIMS = (((1,), (1,)), ((), ()))
TN_DIMS = (((0,), (0,)), ((), ()))


def _params(semantics, est_bytes):
    limit = int(min(max(est_bytes, 16 * 1024 * 1024), VMEM_CAP_BYTES))
    return pltpu.CompilerParams(dimension_semantics=semantics, vmem_limit_bytes=limit)


def _nbytes(shape, dtype):
    return math.prod(shape) * jnp.dtype(dtype).itemsize


def _rmsnorm_kernel(x_ref, g_ref, o_ref):
    x = x_ref[...]
    ms = jnp.mean(x * x, axis=-1, keepdims=True)
    o_ref[...] = (x * lax.rsqrt(ms + EPS) * g_ref[...]).astype(o_ref.dtype)


def rmsnorm(x, g, out_dtype=bf16):
    t, d = x.shape
    tr = min(512, t)
    est = 2 * (_nbytes((tr, d), f32) + _nbytes((tr, d), out_dtype)) + 4 * _nbytes((tr, d), f32)
    return pl.pallas_call(
        _rmsnorm_kernel,
        out_shape=jax.ShapeDtypeStruct((t, d), out_dtype),
        grid=(t // tr,),
        in_specs=[pl.BlockSpec((tr, d), lambda i: (i, 0)), pl.BlockSpec((1, d), lambda i: (0, 0))],
        out_specs=pl.BlockSpec((tr, d), lambda i: (i, 0)),
        compiler_params=_params(("parallel",), est),
        name="rmsnorm",
    )(x, g.reshape(1, d))


def _mm_kernel(x_ref, w_ref, o_ref):
    o_ref[...] = jnp.dot(x_ref[...], w_ref[...], preferred_element_type=f32).astype(o_ref.dtype)


def _mm_res_kernel(x_ref, w_ref, r_ref, o_ref):
    acc = jnp.dot(x_ref[...], w_ref[...], preferred_element_type=f32)
    o_ref[...] = (r_ref[...] + acc).astype(o_ref.dtype)


def matmul(x, w, *, col0=0, n=None, out_dtype=f32, residual=None, tm=1024, tn=1024):
    m, k = x.shape
    n = w.shape[1] - col0 if n is None else n
    tm, tn = min(tm, m), min(tn, n)
    assert col0 % tn == 0 and n % tn == 0 and m % tm == 0
    jb = col0 // tn
    est = 2 * (_nbytes((tm, k), bf16) + _nbytes((k, tn), bf16) + _nbytes((tm, tn), out_dtype)
               + _nbytes((tm, tn), f32)) + _nbytes((tm, tn), f32)
    in_specs = [pl.BlockSpec((tm, k), lambda i, j: (i, 0)), pl.BlockSpec((k, tn), lambda i, j: (0, jb + j))]
    args = [x, w]
    body = _mm_kernel
    if residual is not None:
        in_specs.append(pl.BlockSpec((tm, tn), lambda i, j: (i, j)))
        args.append(residual)
        body = _mm_res_kernel
    return pl.pallas_call(
        body,
        out_shape=jax.ShapeDtypeStruct((m, n), out_dtype),
        grid=(m // tm, n // tn),
        in_specs=in_specs,
        out_specs=pl.BlockSpec((tm, tn), lambda i, j: (i, j)),
        compiler_params=_params(("parallel", "parallel"), est),
        name="matmul",
    )(*args)


def _merge_kernel(hc_ref, at_ref, wc_ref, wa_ref, ga_ref, gb_ref, ba_ref, bb_ref, o_ref):
    conv_out = jnp.dot(hc_ref[...], wc_ref[...], preferred_element_type=f32)
    attn_out = jnp.dot(at_ref[...], wa_ref[...], preferred_element_type=f32)
    gate_a = jax.nn.sigmoid(ga_ref[...] + ba_ref[...])
    gate_b = jax.nn.sigmoid(gb_ref[...] + bb_ref[...])
    o_ref[...] = (gate_a * conv_out + gate_b * attn_out).astype(o_ref.dtype)


def gated_merge(hc, attn, w_conv_out, w_attn_out, proj, b_gate, gate_col0, *, tm=512, tn=512):
    m, k = hc.shape
    n = w_conv_out.shape[1]
    tm, tn = min(tm, m), min(tn, n)
    ga0, gb0 = gate_col0 // tn, (gate_col0 + n) // tn
    nb = n // tn
    est = 2 * (2 * _nbytes((tm, k), bf16) + 2 * _nbytes((k, tn), bf16) + 3 * _nbytes((tm, tn), f32)) \
        + 4 * _nbytes((tm, tn), f32)
    return pl.pallas_call(
        _merge_kernel,
        out_shape=jax.ShapeDtypeStruct((m, n), bf16),
        grid=(m // tm, nb),
        in_specs=[
            pl.BlockSpec((tm, k), lambda i, j: (i, 0)),
            pl.BlockSpec((tm, k), lambda i, j: (i, 0)),
            pl.BlockSpec((k, tn), lambda i, j: (0, j)),
            pl.BlockSpec((k, tn), lambda i, j: (0, j)),
            pl.BlockSpec((tm, tn), lambda i, j: (i, ga0 + j)),
            pl.BlockSpec((tm, tn), lambda i, j: (i, gb0 + j)),
            pl.BlockSpec((1, tn), lambda i, j: (0, j)),
            pl.BlockSpec((1, tn), lambda i, j: (0, nb + j)),
        ],
        out_specs=pl.BlockSpec((tm, tn), lambda i, j: (i, j)),
        compiler_params=_params(("parallel", "parallel"), est),
        name="gated_merge",
    )(hc, attn, w_conv_out, w_attn_out, proj, proj, b_gate, b_gate)


CONV_ROW_CHUNK = 64
CONV_LANE_CHUNK = 256
CONV_HALO = 32


def _conv_kernel(a_ref, gt_ref, pre_ref, wdw_ref, bdw_ref, lng_ref, lnb_ref, hc_ref, st_ref, ubuf, ybuf,
                 *, tt, width):
    t = pl.program_id(1)
    c = ubuf.shape[1]
    lead = CONV_HALO - (width - 1)
    rc = min(CONV_ROW_CHUNK, tt)

    @pl.when(t == 0)
    def _():
        ubuf[0:lead, :] = jnp.zeros((lead, c), f32)
        ubuf[lead:CONV_HALO, :] = pre_ref[...]

    ubuf[CONV_HALO:CONV_HALO + tt, :] = a_ref[...] * jax.nn.sigmoid(gt_ref[...])

    n_lane = c // CONV_LANE_CHUNK

    def chunk(idx, carry):
        r0 = pl.multiple_of((idx // n_lane) * rc, rc)
        c0 = pl.multiple_of((idx % n_lane) * CONV_LANE_CHUNK, CONV_LANE_CHUNK)
        win = ubuf[pl.ds(r0, rc + CONV_HALO), pl.ds(c0, CONV_LANE_CHUNK)]
        acc = jnp.broadcast_to(bdw_ref[:, pl.ds(c0, CONV_LANE_CHUNK)], (rc, CONV_LANE_CHUNK))
        for phase in range(V7X_SUBLANES):
            offs = [o for o in range(lead, lead + width) if o % V7X_SUBLANES == phase]
            if not offs:
                continue
            shifted = win if phase == 0 else pltpu.roll(win, rc + CONV_HALO - phase, 0)
            for o in offs:
                q = o - phase
                w_row = wdw_ref[o - lead:o - lead + 1, pl.ds(c0, CONV_LANE_CHUNK)]
                acc = acc + shifted[q:q + rc] * w_row
        ybuf[pl.ds(r0, rc), pl.ds(c0, CONV_LANE_CHUNK)] = acc
        return carry

    lax.fori_loop(0, (tt // rc) * n_lane, chunk, 0)

    def ln_chunk(i, carry):
        r0 = pl.multiple_of(i * rc, rc)
        y = ybuf[pl.ds(r0, rc), :]
        mu = jnp.mean(y, axis=-1, keepdims=True)
        yc = y - mu
        var = jnp.mean(yc * yc, axis=-1, keepdims=True)
        yn = yc * lax.rsqrt(var + EPS) * lng_ref[...] + lnb_ref[...]
        hc_ref[pl.ds(r0, rc), :] = (yn * jax.nn.sigmoid(yn)).astype(hc_ref.dtype)
        return carry

    lax.fori_loop(0, tt // rc, ln_chunk, 0)

    @pl.when(t == pl.num_programs(1) - 1)
    def _():
        st_ref[...] = ubuf[tt + lead:tt + CONV_HALO, :]

    ubuf[0:CONV_HALO, :] = ubuf[tt:tt + CONV_HALO, :]


def conformer_conv(proj, prefix, w_dw, b_dw, ln_g, ln_b, *, n_seq, seq_len, tt, out_dtype):
    width, c = w_dw.shape
    nt = seq_len // tt
    est = 2 * (2 * _nbytes((tt, c), f32) + _nbytes((width - 1, c), f32) + _nbytes((width, c), f32)
               + _nbytes((tt, c), out_dtype) + _nbytes((width - 1, c), f32)) \
        + _nbytes((2 * tt + CONV_HALO, c), f32) + 8 * _nbytes((CONV_ROW_CHUNK, c), f32)
    kern = functools.partial(_conv_kernel, tt=tt, width=width)
    return pl.pallas_call(
        kern,
        out_shape=(jax.ShapeDtypeStruct((n_seq * seq_len, c), out_dtype),
                   jax.ShapeDtypeStruct((n_seq, width - 1, c), f32)),
        grid=(n_seq, nt),
        in_specs=[
            pl.BlockSpec((tt, c), lambda b, t: (b * nt + t, 0)),
            pl.BlockSpec((tt, c), lambda b, t: (b * nt + t, 1)),
            pl.BlockSpec((None, width - 1, c), lambda b, t: (b, 0, 0)),
            pl.BlockSpec((width, c), lambda b, t: (0, 0)),
            pl.BlockSpec((1, c), lambda b, t: (0, 0)),
            pl.BlockSpec((1, c), lambda b, t: (0, 0)),
            pl.BlockSpec((1, c), lambda b, t: (0, 0)),
        ],
        out_specs=(pl.BlockSpec((tt, c), lambda b, t: (b * nt + t, 0)),
                   pl.BlockSpec((None, width - 1, c), lambda b, t: (b, 0, 0))),
        scratch_shapes=[pltpu.VMEM((tt + CONV_HALO, c), f32), pltpu.VMEM((tt, c), f32)],
        compiler_params=_params(("parallel", "arbitrary"), est),
        name="conformer_conv",
    )(proj, proj, prefix, w_dw, b_dw.reshape(1, c), ln_g.reshape(1, c), ln_b.reshape(1, c))


def _rel_bucket(dist):
    n = jnp.maximum(dist, 0)
    max_exact = NUM_BUCKETS // 2
    large = max_exact + (jnp.log(jnp.maximum(n, 1).astype(f32) / max_exact)
                         / math.log(MAX_DISTANCE / max_exact) * (NUM_BUCKETS - max_exact)).astype(jnp.int32)
    large = jnp.minimum(large, NUM_BUCKETS - 1)
    return jnp.where(n < max_exact, n, large)


def _bias_from_dist(dist, lookup):
    bucket = _rel_bucket(dist)
    out = jnp.zeros(dist.shape, f32)
    for b in range(NUM_BUCKETS):
        out = jnp.where(bucket == b, lookup(b), out)
    return jnp.where(dist >= 0, out, NEG)


def _prompt_bias_kernel(rb_ref, o_ref):
    h = pl.program_id(0)
    ki = lax.broadcasted_iota(jnp.int32, (MOBA_BLOCK, MOBA_BLOCK), 0)
    qi = lax.broadcasted_iota(jnp.int32, (MOBA_BLOCK, MOBA_BLOCK), 1)
    for tab in range(3):
        dist = tab * MOBA_BLOCK + qi - ki
        o_ref[tab] = _bias_from_dist(dist, lambda b: rb_ref[b, h])


def prompt_bias_tables(rel_bias):
    n_heads = rel_bias.shape[1]
    return pl.pallas_call(
        _prompt_bias_kernel,
        out_shape=jax.ShapeDtypeStruct((n_heads, 3, MOBA_BLOCK, MOBA_BLOCK), f32),
        grid=(n_heads,),
        in_specs=[pl.BlockSpec(memory_space=pltpu.SMEM)],
        out_specs=pl.BlockSpec((None, 3, MOBA_BLOCK, MOBA_BLOCK), lambda h: (h, 0, 0, 0)),
        compiler_params=_params(("parallel",), 16 * 1024 * 1024),
        name="prompt_bias_tables",
    )(rel_bias)


def _sample_bias_kernel(rb_ref, page_ref, own_ref, *, past_len, page, n_q, n_near, n_heads):
    n_rows = n_heads * n_q
    rowh = lax.broadcasted_iota(jnp.int32, (n_rows, 1), 0) // n_q

    def lookup(b):
        out = jnp.zeros((n_rows, 1), f32)
        for hh in range(n_heads):
            out = jnp.where(rowh == hh, rb_ref[b, hh], out)
        return out

    def table(width, col_head, dist):
        row = lax.broadcasted_iota(jnp.int32, (n_rows, width), 0)
        col = lax.broadcasted_iota(jnp.int32, (n_rows, width), 1)
        same_head = (row // n_q) == col_head(col)
        return jnp.where(same_head, _bias_from_dist(dist(row % n_q, col), lookup), NEG)

    n_pages = past_len // page
    wide = page * n_heads
    page_ref[0] = table(wide, lambda c: c % n_heads, lambda q, c: jnp.full(c.shape, MAX_DISTANCE, jnp.int32))
    for i in range(n_near):
        pg = n_pages - n_near + i
        page_ref[1 + i] = table(wide, lambda c: c % n_heads,
                                lambda q, c: past_len + q - (pg * page + c // n_heads))
    own_ref[...] = table(n_rows, lambda c: c // n_q, lambda q, c: q - c % n_q)


def sample_bias_tables(rel_bias, *, past_len, page, n_q, n_near):
    n_heads = rel_bias.shape[1]
    n_rows = n_heads * n_q
    kern = functools.partial(_sample_bias_kernel, past_len=past_len, page=page, n_q=n_q, n_near=n_near,
                             n_heads=n_heads)
    return pl.pallas_call(
        kern,
        out_shape=(jax.ShapeDtypeStruct((n_near + 1, n_rows, page * n_heads), f32),
                   jax.ShapeDtypeStruct((n_rows, n_rows), f32)),
        in_specs=[pl.BlockSpec(memory_space=pltpu.SMEM)],
        out_specs=(pl.BlockSpec(memory_space=pltpu.VMEM), pl.BlockSpec(memory_space=pltpu.VMEM)),
        name="sample_bias_tables",
    )(rel_bias)


MOBA_HEAD_GROUP = 4


def _moba_prompt_kernel(q_ref, k_ref, v_ref, bias_ref, o_ref,
                        kb_ref, vt_ref, mean_ref, qb_ref, sel_ref, m_ref, l_ref, acc_ref,
                        s_ref, p_ref, alpha_ref, *, head_dim):
    c = pl.program_id(2)
    n_blocks = k_ref.shape[0] // MOBA_BLOCK
    scale = head_dim ** -0.5

    for hh in range(MOBA_HEAD_GROUP):
        lanes = slice(hh * head_dim, (hh + 1) * head_dim)

        @pl.when(c == 0)
        def _():
            kf = k_ref[:, lanes]
            kb_ref[hh] = kf.astype(bf16)
            vt_ref[hh] = v_ref[:, lanes].T.astype(bf16)
            mean_ref[hh] = jnp.mean(kf.reshape(n_blocks, MOBA_BLOCK, head_dim), axis=1)

        qf = q_ref[:, lanes]
        st = lax.dot_general(mean_ref[hh], qf, NT_DIMS, preferred_element_type=f32,
                             precision=lax.Precision.HIGHEST)
        row = lax.broadcasted_iota(jnp.int32, st.shape, 0)
        sm = jnp.where(row < c, st, NEG)
        for j in range(n_blocks):
            rj = sm[j:j + 1, :]
            beats = jnp.where((sm > rj) | ((sm == rj) & (row < j)), 1.0, 0.0)
            rank = jnp.sum(beats, axis=0, keepdims=True)
            jv = jnp.full(rank.shape, j, jnp.int32)
            chosen = jnp.where((rank < MOBA_TOPK) & (jv < c), 1.0, 0.0)
            sel_ref[hh, j] = jnp.where(jv == c, 1.0, chosen)

        qb_ref[hh] = qf.astype(bf16)
        m_ref[hh] = jnp.full(m_ref.shape[1:], NEG, f32)
        l_ref[hh] = jnp.zeros(l_ref.shape[1:], f32)
        acc_ref[hh] = jnp.zeros(acc_ref.shape[1:], f32)

    def tile(i, carry):
        j = c - i
        tab = jnp.minimum(i, 2)
        k0 = pl.multiple_of(j * MOBA_BLOCK, MOBA_BLOCK)
        for hh in range(MOBA_HEAD_GROUP):
            kj = kb_ref[hh, pl.ds(k0, MOBA_BLOCK), :]
            s_ref[hh] = lax.dot_general(kj, qb_ref[hh], NT_DIMS, preferred_element_type=f32)
        for hh in range(MOBA_HEAD_GROUP):
            lg = s_ref[hh] * scale + bias_ref[hh, tab]
            lg = jnp.where(sel_ref[hh, j] > 0.5, lg, NEG)
            m_old = m_ref[hh]
            m_new = jnp.maximum(m_old, jnp.max(lg, axis=0, keepdims=True))
            p = jnp.exp(lg - m_new)
            alpha = jnp.exp(m_old - m_new)
            l_ref[hh] = alpha * l_ref[hh] + jnp.sum(p, axis=0, keepdims=True)
            p_ref[hh] = p.astype(bf16)
            alpha_ref[hh] = alpha
            m_ref[hh] = m_new
        for hh in range(MOBA_HEAD_GROUP):
            vtj = vt_ref[hh, :, pl.ds(k0, MOBA_BLOCK)]
            acc_ref[hh] = alpha_ref[hh] * acc_ref[hh] + jnp.dot(vtj, p_ref[hh], preferred_element_type=f32)
        return carry

    lax.fori_loop(0, c + 1, tile, 0)
    for hh in range(MOBA_HEAD_GROUP):
        o_ref[:, hh * head_dim:(hh + 1) * head_dim] = (acc_ref[hh] / l_ref[hh]).T.astype(o_ref.dtype)


def moba_prompt(proj, k, v, bias_tabs, *, n_seq, seq_len, n_heads, head_dim, q_col0):
    width = n_heads * head_dim
    gw = MOBA_HEAD_GROUP * head_dim
    n_groups = n_heads // MOBA_HEAD_GROUP
    n_blocks = seq_len // MOBA_BLOCK
    qb0 = q_col0 // gw
    est = 2 * (2 * _nbytes((MOBA_BLOCK, gw), f32) + 2 * _nbytes((seq_len, gw), f32)
               + _nbytes((MOBA_HEAD_GROUP, 3, MOBA_BLOCK, MOBA_BLOCK), f32)) \
        + 2 * _nbytes((MOBA_HEAD_GROUP, seq_len, head_dim), bf16) + 8 * _nbytes((MOBA_BLOCK, MOBA_BLOCK), f32)
    kern = functools.partial(_moba_prompt_kernel, head_dim=head_dim)
    return pl.pallas_call(
        kern,
        out_shape=jax.ShapeDtypeStruct((n_seq * seq_len, width), bf16),
        grid=(n_seq, n_groups, n_blocks),
        in_specs=[
            pl.BlockSpec((MOBA_BLOCK, gw), lambda b, g, c: (b * n_blocks + c, qb0 + g)),
            pl.BlockSpec((seq_len, gw), lambda b, g, c: (b, g)),
            pl.BlockSpec((seq_len, gw), lambda b, g, c: (b, g)),
            pl.BlockSpec((MOBA_HEAD_GROUP, 3, MOBA_BLOCK, MOBA_BLOCK), lambda b, g, c: (g, 0, 0, 0)),
        ],
        out_specs=pl.BlockSpec((MOBA_BLOCK, gw), lambda b, g, c: (b * n_blocks + c, g)),
        scratch_shapes=[
            pltpu.VMEM((MOBA_HEAD_GROUP, seq_len, head_dim), bf16),
            pltpu.VMEM((MOBA_HEAD_GROUP, head_dim, seq_len), bf16),
            pltpu.VMEM((MOBA_HEAD_GROUP, n_blocks, head_dim), f32),
            pltpu.VMEM((MOBA_HEAD_GROUP, MOBA_BLOCK, head_dim), bf16),
            pltpu.VMEM((MOBA_HEAD_GROUP, n_blocks, 1, MOBA_BLOCK), f32),
            pltpu.VMEM((MOBA_HEAD_GROUP, 1, MOBA_BLOCK), f32),
            pltpu.VMEM((MOBA_HEAD_GROUP, 1, MOBA_BLOCK), f32),
            pltpu.VMEM((MOBA_HEAD_GROUP, head_dim, MOBA_BLOCK), f32),
            pltpu.VMEM((MOBA_HEAD_GROUP, MOBA_BLOCK, MOBA_BLOCK), f32),
            pltpu.VMEM((MOBA_HEAD_GROUP, MOBA_BLOCK, MOBA_BLOCK), bf16),
            pltpu.VMEM((MOBA_HEAD_GROUP, 1, MOBA_BLOCK), f32),
        ],
        compiler_params=_params(("parallel", "parallel", "arbitrary"), est),
        name="moba_prompt",
    )(proj, k, v, bias_tabs)


def _moba_sample_kernel(pt_ref, q_ref, kn_ref, vn_ref, *rest,
                        n_heads, head_dim, n_q, n_near, n_pages, pages_per_block):
    del pt_ref
    kc_refs = rest[:pages_per_block]
    vc_refs = rest[pages_per_block:2 * pages_per_block]
    tab_ref, own_ref, o_ref, qb_ref, qf_ref, own_kv_ref, score_ref, m_ref, l_ref, part_ref = rest[2 * pages_per_block:]
    s = pl.program_id(1)
    n_blocks = n_pages // pages_per_block
    page = kc_refs[0].shape[0]
    n_rows = n_heads * n_q
    scale = head_dim ** -0.5
    lane = lax.broadcasted_iota(jnp.int32, (n_rows, V7X_LANES), 1)

    @pl.when(s == 0)
    def _():
        for h in range(n_heads):
            qf_ref[h * n_q:(h + 1) * n_q, :] = q_ref[:, h * head_dim:(h + 1) * head_dim]
        qb_ref[...] = qf_ref[...].astype(bf16)
        score_ref[...] = jnp.full(score_ref.shape, -jnp.inf, f32)
        m_ref[...] = jnp.full(m_ref.shape, NEG, f32)
        l_ref[...] = jnp.zeros(l_ref.shape, f32)

    def record(ref, slot, col):
        ref[...] = jnp.where(lane == slot, col, ref[...])

    def partial_softmax(slots, logits, values):
        ps = []
        for slot, lg in zip(slots, logits):
            m = jnp.max(lg, axis=1, keepdims=True)
            p = jnp.exp(lg - m)
            record(m_ref, slot, m)
            record(l_ref, slot, jnp.sum(p, axis=1, keepdims=True))
            ps.append(p.astype(bf16))
        for slot, p, v in zip(slots, ps, values):
            part_ref[slot] = jnp.dot(p, v, preferred_element_type=f32)

    @pl.when(s < n_blocks)
    def _():
        slots, logits, values = [], [], []
        ksum = jnp.zeros((n_heads, head_dim), f32)
        for i in range(pages_per_block):
            pg = s * pages_per_block + i
            k3 = kc_refs[i][...]
            ksum = ksum + jnp.sum(k3, axis=0)
            k2 = k3.reshape(page * n_heads, head_dim).astype(bf16)
            raw = lax.dot_general(qb_ref[...], k2, NT_DIMS, preferred_element_type=f32)
            near = pg - (n_pages - n_near)
            slots.append(pg)
            logits.append(raw * scale + tab_ref[jnp.maximum(near + 1, 0)])
            values.append(vc_refs[i][...].reshape(page * n_heads, head_dim).astype(bf16))
        ksum_rows = jnp.concatenate(
            [jnp.broadcast_to(ksum[h:h + 1, :], (n_q, head_dim)) for h in range(n_heads)], axis=0)
        record(score_ref, s, jnp.sum(qf_ref[...] * ksum_rows, axis=1, keepdims=True))
        partial_softmax(slots, logits, values)

    @pl.when(s == n_blocks)
    def _():
        for h in range(n_heads):
            own_kv_ref[0, h * n_q:(h + 1) * n_q, :] = kn_ref[:, h * head_dim:(h + 1) * head_dim]
            own_kv_ref[1, h * n_q:(h + 1) * n_q, :] = vn_ref[:, h * head_dim:(h + 1) * head_dim]
        raw = lax.dot_general(qb_ref[...], own_kv_ref[0].astype(bf16), NT_DIMS, preferred_element_type=f32)
        partial_softmax([n_pages], [raw * scale + own_ref[...]], [own_kv_ref[1].astype(bf16)])

        score = score_ref[...]
        sel = jnp.where(lane == n_pages, 1.0, 0.0)
        for _ in range(MOBA_TOPK):
            mx = jnp.max(score, axis=1, keepdims=True)
            first = jnp.min(jnp.where(score == mx, lane, V7X_LANES), axis=1, keepdims=True)
            sel = jnp.where((lane // pages_per_block == first) & (lane < n_pages), 1.0, sel)
            score = jnp.where(lane == first, -jnp.inf, score)

        m_all, l_all = m_ref[...], l_ref[...]
        m_top = jnp.max(jnp.where(sel > 0.5, m_all, NEG), axis=1, keepdims=True)
        w = jnp.where(sel > 0.5, jnp.exp(m_all - m_top), 0.0)
        wn = w / jnp.sum(w * l_all, axis=1, keepdims=True)
        out = jnp.zeros((n_rows, head_dim), f32)
        for i in range(n_pages + 1):
            out = out + wn[:, i:i + 1] * part_ref[i]
        for h in range(n_heads):
            o_ref[:, h * head_dim:(h + 1) * head_dim] = out[h * n_q:(h + 1) * n_q, :].astype(o_ref.dtype)


def moba_sample(proj, k_new, v_new, cache_k, cache_v, page_table, page_tabs, own_tab, *,
                layer, n_seq, n_q, q_col0, n_near):
    _, _, page, n_heads, head_dim = cache_k.shape
    width = n_heads * head_dim
    n_rows = n_heads * n_q
    n_pages = page_table.shape[1]
    pages_per_block = MOBA_BLOCK // page
    n_blocks = n_pages // pages_per_block
    qb0 = q_col0 // width
    assert n_rows == V7X_LANES and MOBA_TOPK <= n_blocks and n_pages < V7X_LANES and n_pages % pages_per_block == 0
    page_bytes = _nbytes((page, n_heads, head_dim), f32)
    est = 2 * (4 * _nbytes((n_q, width), f32) + 2 * pages_per_block * page_bytes
               + _nbytes(page_tabs.shape, f32) + _nbytes(own_tab.shape, f32)) \
        + _nbytes((n_pages + 1, n_rows, head_dim), f32) + 8 * pages_per_block * page_bytes
    kern = functools.partial(_moba_sample_kernel, n_heads=n_heads, head_dim=head_dim, n_q=n_q,
                             n_near=n_near, n_pages=n_pages, pages_per_block=pages_per_block)

    def page_spec(i):
        def index(n, s, pt):
            return (layer, pt[n, jnp.minimum(s * pages_per_block + i, n_pages - 1)], 0, 0, 0)
        return pl.BlockSpec((None, None, page, n_heads, head_dim), index)

    pages = [page_spec(i) for i in range(pages_per_block)]
    stat = pltpu.VMEM((n_rows, V7X_LANES), f32)
    grid_spec = pltpu.PrefetchScalarGridSpec(
        num_scalar_prefetch=1,
        grid=(n_seq, n_blocks + 1),
        in_specs=[
            pl.BlockSpec((n_q, width), lambda n, s, pt: (n, qb0)),
            pl.BlockSpec((n_q, width), lambda n, s, pt: (n, 0)),
            pl.BlockSpec((n_q, width), lambda n, s, pt: (n, 0)),
            *pages, *pages,
            pl.BlockSpec(page_tabs.shape, lambda n, s, pt: (0, 0, 0)),
            pl.BlockSpec(own_tab.shape, lambda n, s, pt: (0, 0)),
        ],
        out_specs=pl.BlockSpec((n_q, width), lambda n, s, pt: (n, 0)),
        scratch_shapes=[
            pltpu.VMEM((n_rows, head_dim), bf16),
            pltpu.VMEM((n_rows, head_dim), f32),
            pltpu.VMEM((2, n_rows, head_dim), f32),
            stat, stat, stat,
            pltpu.VMEM((n_pages + 1, n_rows, head_dim), f32),
        ],
    )
    return pl.pallas_call(
        kern,
        out_shape=jax.ShapeDtypeStruct((n_seq * n_q, width), f32),
        grid_spec=grid_spec,
        compiler_params=_params(("parallel", "arbitrary"), est),
        name="moba_sample",
    )(page_table, proj, k_new, v_new, *([cache_k] * pages_per_block), *([cache_v] * pages_per_block),
      page_tabs, own_tab)


def _peer_score_kernel(q_ref, sk_ref, ns1_ref, t2_ref, e1_ref, e2_ref, *, n_heads, n_keys):
    dk = sk_ref.shape[3]
    neg_inf = -jnp.inf
    n_top = PEER_TOPK + 1
    pairs = [(i, j) for i in range(n_top) for j in range(n_top) if (i + 1) * (j + 1) <= n_top]
    n_cand = -(-len(pairs) // V7X_SUBLANES) * V7X_SUBLANES
    for h in range(n_heads):
        scores, tops = [], []
        for part in range(2):
            c0 = (2 * h + part) * dk
            s = lax.dot_general(sk_ref[h, part], q_ref[:, c0:c0 + dk], NT_DIMS, preferred_element_type=f32)
            scores.append(s)
            vals, w = [], s
            for _ in range(n_top):
                mx = jnp.max(w, axis=0, keepdims=True)
                vals.append(mx)
                w = jnp.where(w == mx, neg_inf, w)
            tops.append(vals)
        a, b = tops
        tm = a[0].shape[1]
        cand = jnp.concatenate([a[i] + b[j] for i, j in pairs]
                               + [jnp.full((n_cand - len(pairs), tm), neg_inf, f32)], axis=0)
        rowi = lax.broadcasted_iota(jnp.int32, cand.shape, 0)
        best = []
        for _ in range(n_top):
            mx = jnp.max(cand, axis=0, keepdims=True)
            first = jnp.min(jnp.where(cand == mx, rowi, n_cand), axis=0, keepdims=True)
            best.append(mx)
            cand = jnp.where(rowi == first, neg_inf, cand)
        z = jnp.zeros_like(best[0])
        for c in best[:PEER_TOPK]:
            z = z + jnp.exp(c - best[0])
        theta = 0.5 * (best[PEER_TOPK - 1] + best[PEER_TOPK])
        ns1 = -scores[0]
        t2 = scores[1] - theta
        e1 = jnp.exp(scores[0] - a[0])
        e2 = jnp.exp(scores[1] - b[0]) / z
        for c in range(tm // V7X_LANES):
            cols = slice(c * V7X_LANES, (c + 1) * V7X_LANES)
            ns1_ref[h, c] = ns1[:, cols]
            t2_ref[h, c] = t2[:, cols]
            e1_ref[h, c] = e1[:, cols]
            e2_ref[h, c] = e2[:, cols]


def peer_scores(qp, subkeys, *, tm):
    t = qp.shape[0]
    n_heads, _, n_keys, dk = subkeys.shape
    tm = min(tm, t)
    big = jax.ShapeDtypeStruct((n_heads, t // V7X_LANES, n_keys, V7X_LANES), f32)
    blk = pl.BlockSpec((n_heads, tm // V7X_LANES, n_keys, V7X_LANES), lambda i: (0, i, 0, 0))
    est = 2 * (_nbytes((tm, qp.shape[1]), bf16) + _nbytes(subkeys.shape, bf16) + 4 * _nbytes((n_heads, n_keys, tm), f32)) \
        + 16 * _nbytes((n_keys, tm), f32)
    kern = functools.partial(_peer_score_kernel, n_heads=n_heads, n_keys=n_keys)
    return pl.pallas_call(
        kern,
        out_shape=(big, big, big, big),
        grid=(t // tm,),
        in_specs=[pl.BlockSpec((tm, qp.shape[1]), lambda i: (i, 0)),
                  pl.BlockSpec(subkeys.shape, lambda i: (0, 0, 0, 0))],
        out_specs=(blk, blk, blk, blk),
        compiler_params=_params(("parallel",), est),
        name="peer_scores",
    )(qp, subkeys)


def _peer_dense_kernel(x_ref, u_ref, v_ref, ns1_ref, t2_ref, e1_ref, e2_ref, h_ref, g_ref, o_ref,
                       acc_ref, act_ref, wact_ref, *, n_heads, n_keys):
    e = pl.program_id(1)
    te, tm = act_ref.shape
    rows_per_step = te // n_keys

    @pl.when(e == 0)
    def _():
        acc_ref[...] = jnp.zeros(acc_ref.shape, f32)

    act_ref[...] = lax.dot_general(u_ref[...], x_ref[...], NT_DIMS, preferred_element_type=f32)

    for il in range(rows_per_step):
        i1 = e * rows_per_step + il
        rows = slice(il * n_keys, (il + 1) * n_keys)
        for c in range(tm // V7X_LANES):
            cols = slice(c * V7X_LANES, (c + 1) * V7X_LANES)
            wt = jnp.zeros((n_keys, V7X_LANES), f32)
            for h in range(n_heads):
                chosen = t2_ref[h, c] >= ns1_ref[h, c, pl.ds(i1, 1), :]
                weight = e2_ref[h, c] * e1_ref[h, c, pl.ds(i1, 1), :]
                wt = wt + jnp.where(chosen, weight, 0.0)
            wact_ref[cols, rows] = (wt * jax.nn.gelu(act_ref[rows, cols])).T.astype(bf16)

    acc_ref[...] += jnp.dot(wact_ref[...], v_ref[...], preferred_element_type=f32)

    @pl.when(e == pl.num_programs(1) - 1)
    def _():
        r = h_ref[...] + acc_ref[...]
        ms = jnp.mean(r * r, axis=-1, keepdims=True)
        o_ref[...] = r * lax.rsqrt(ms + EPS) * g_ref[...]


def peer_dense(xn, u_tab, v_tab, ns1, t2, e1, e2, h, g, *, tm, te):
    t, d = xn.shape
    n_exp = u_tab.shape[0]
    n_heads, _, n_keys, _ = ns1.shape
    tm = min(tm, t)
    once = dict(pipeline_mode=pl.Buffered(1))
    sblk = pl.BlockSpec((n_heads, tm // V7X_LANES, n_keys, V7X_LANES), lambda i, e: (0, i, 0, 0), **once)
    est = _nbytes((tm, d), bf16) + 4 * _nbytes((te, d), bf16) + 4 * _nbytes((n_heads, n_keys, tm), f32) \
        + 4 * _nbytes((tm, d), f32) + 2 * _nbytes((te, tm), f32) + 8 * _nbytes((te, tm), f32)
    kern = functools.partial(_peer_dense_kernel, n_heads=n_heads, n_keys=n_keys)
    return pl.pallas_call(
        kern,
        out_shape=jax.ShapeDtypeStruct((t, d), f32),
        grid=(t // tm, n_exp // te),
        in_specs=[
            pl.BlockSpec((tm, d), lambda i, e: (i, 0), **once),
            pl.BlockSpec((te, d), lambda i, e: (e, 0)),
            pl.BlockSpec((te, d), lambda i, e: (e, 0)),
            sblk, sblk, sblk, sblk,
            pl.BlockSpec((tm, d), lambda i, e: (i, 0), **once),
            pl.BlockSpec((1, d), lambda i, e: (0, 0)),
        ],
        out_specs=pl.BlockSpec((tm, d), lambda i, e: (i, 0)),
        scratch_shapes=[pltpu.VMEM((tm, d), f32), pltpu.VMEM((te, tm), f32), pltpu.VMEM((tm, te), bf16)],
        compiler_params=_params(("parallel", "arbitrary"), est),
        name="peer_dense",
    )(xn, u_tab, v_tab, ns1, t2, e1, e2, h, g.reshape(1, d))


def _layer(x, prefix, attend, conv_tt, conv_dtype, lw, norm_final_g, *, n_seq, seq_len, peer_tm, peer_te):
    (norm_mix_g, w_in, b_gate, conv_w_dw, conv_b_dw, conv_ln_g, conv_ln_b, w_conv_out, w_attn_out, w_out,
     norm_ffn_g, peer_wq, peer_subkeys, peer_u, peer_v) = lw
    d = x.shape[1]
    conv_ch = conv_w_dw.shape[1]
    attn_w = w_attn_out.shape[0]
    q_col0 = 2 * conv_ch
    gate_col0 = q_col0 + 3 * attn_w

    xn = rmsnorm(x, norm_mix_g)
    proj = matmul(xn, w_in, col0=0, n=q_col0 + attn_w)
    k = matmul(xn, w_in, col0=q_col0 + attn_w, n=attn_w)
    v = matmul(xn, w_in, col0=q_col0 + 2 * attn_w, n=attn_w)
    gates = matmul(xn, w_in, col0=gate_col0)
    hc, conv_state = conformer_conv(proj, prefix, conv_w_dw, conv_b_dw, conv_ln_g, conv_ln_b,
                                    n_seq=n_seq, seq_len=seq_len, tt=conv_tt, out_dtype=conv_dtype)
    attn = attend(proj, k, v, q_col0)
    merged = gated_merge(hc.astype(bf16), attn.astype(bf16), w_conv_out, w_attn_out, gates,
                         b_gate.reshape(1, -1), 0)
    h = matmul(merged, w_out, residual=x)
    xn2 = rmsnorm(h, norm_ffn_g)
    qp = matmul(xn2, peer_wq, out_dtype=bf16)
    ns1, t2, e1, e2 = peer_scores(qp, peer_subkeys, tm=peer_tm)
    y = peer_dense(xn2, peer_u, peer_v, ns1, t2, e1, e2, h, norm_final_g, tm=peer_tm, te=peer_te)
    return y, k, v, conv_state


def kernel(x_prompt, x_sample, cache_k, cache_v, state_conv, page_table, rel_bias, norm_mix_g, w_in, b_gate,
           conv_w_dw, conv_b_dw, conv_ln_g, conv_ln_b, w_conv_out, w_attn_out, w_out, norm_ffn_g,
           peer_wq, peer_subkeys, peer_u, peer_v, norm_final_g):
    depth = w_in.shape[0]
    assert depth == 1, "the final norm is fused into the last layer's PEER kernel"
    n_p, seq_p, d = x_prompt.shape
    n_s, seq_s, _ = x_sample.shape
    _, n_pool, page, n_heads, head_dim = cache_k.shape
    width = n_heads * head_dim
    past_len = page_table.shape[1] * page
    conv_w = conv_w_dw.shape[1]
    conv_ch = conv_w_dw.shape[2]

    n_near = sum(1 for pg in range(past_len // page) if past_len - (pg * page + page - 1) < MAX_DISTANCE)

    l = 0
    lw = (norm_mix_g[l], w_in[l].astype(bf16), b_gate[l], conv_w_dw[l], conv_b_dw[l], conv_ln_g[l], conv_ln_b[l],
          w_conv_out[l].astype(bf16), w_attn_out[l].astype(bf16), w_out[l].astype(bf16), norm_ffn_g[l],
          peer_wq[l].astype(bf16), peer_subkeys[l].astype(bf16), peer_u[l].astype(bf16), peer_v[l].astype(bf16))

    p_tabs = prompt_bias_tables(rel_bias)
    page_tabs, own_tab = sample_bias_tables(rel_bias, past_len=past_len, page=page, n_q=seq_s, n_near=n_near)

    attend_p = lambda proj, k, v, q_col0: moba_prompt(proj, k, v, p_tabs, n_seq=n_p, seq_len=seq_p,
                                                      n_heads=n_heads, head_dim=head_dim, q_col0=q_col0)
    attend_s = lambda proj, k, v, q_col0: moba_sample(proj, k, v, cache_k, cache_v, page_table, page_tabs, own_tab,
                                                      layer=l, n_seq=n_s, n_q=seq_s, q_col0=q_col0, n_near=n_near)

    zeros_prefix = jnp.zeros((n_p, conv_w - 1, conv_ch), f32)
    yp, kp, vp, cp = _layer(x_prompt.reshape(n_p * seq_p, d), zeros_prefix, attend_p, 256, bf16, lw, norm_final_g,
                            n_seq=n_p, seq_len=seq_p, peer_tm=512, peer_te=1024)
    ys, ks, vs, cs = _layer(x_sample.reshape(n_s * seq_s, d), state_conv[l], attend_s, seq_s, f32, lw, norm_final_g,
                            n_seq=n_s, seq_len=seq_s, peer_tm=256, peer_te=512)

    return (yp.reshape(n_p, seq_p, d), ys.reshape(n_s, seq_s, d),
            kp.reshape(1, n_p, seq_p, n_heads, head_dim), vp.reshape(1, n_p, seq_p, n_heads, head_dim),
            ks.reshape(1, n_s, seq_s, n_heads, head_dim), vs.reshape(1, n_s, seq_s, n_heads, head_dim),
            cp[None], cs[None])
```

```python
import functools
import math

import jax
import jax.numpy as jnp
from jax import lax
from jax.experimental import pallas as pl
from jax.experimental.pallas import tpu as pltpu

f32 = jnp.float32
bf16 = jnp.bfloat16

MOBA_BLOCK = 256
MOBA_TOPK = 3
NUM_BUCKETS = 32
MAX_DISTANCE = 128
PEER_TOPK = 16
EPS = 1e-6
NEG = -1e30

V7X_VMEM_BYTES = 64 * 1024 * 1024
V7X_LANES = 128
V7X_SUBLANES = 8
VMEM_CAP_BYTES = V7X_VMEM_BYTES - 8 * 1024 * 1024

NT_DIMS = (((1,), (1,)), ((), ()))
TN_DIMS = (((0,), (0,)), ((), ()))


def _params(semantics, est_bytes):
    limit = int(min(max(est_bytes, 16 * 1024 * 1024), VMEM_CAP_BYTES))
    return pltpu.CompilerParams(dimension_semantics=semantics, vmem_limit_bytes=limit)


def _nbytes(shape, dtype):
    return math.prod(shape) * jnp.dtype(dtype).itemsize


def _rmsnorm_kernel(x_ref, g_ref, o_ref):
    x = x_ref[...]
    ms = jnp.mean(x * x, axis=-1, keepdims=True)
    o_ref[...] = (x * lax.rsqrt(ms + EPS) * g_ref[...]).astype(o_ref.dtype)


def rmsnorm(x, g, out_dtype=bf16):
    t, d = x.shape
    tr = min(512, t)
    est = 2 * (_nbytes((tr, d), f32) + _nbytes((tr, d), out_dtype)) + 4 * _nbytes((tr, d), f32)
    return pl.pallas_call(
        _rmsnorm_kernel,
        out_shape=jax.ShapeDtypeStruct((t, d), out_dtype),
        grid=(t // tr,),
        in_specs=[pl.BlockSpec((tr, d), lambda i: (i, 0)), pl.BlockSpec((1, d), lambda i: (0, 0))],
        out_specs=pl.BlockSpec((tr, d), lambda i: (i, 0)),
        compiler_params=_params(("parallel",), est),
        name="rmsnorm",
    )(x, g.reshape(1, d))


def _mm_kernel(x_ref, w_ref, o_ref):
    o_ref[...] = jnp.dot(x_ref[...], w_ref[...], preferred_element_type=f32).astype(o_ref.dtype)


def _mm_res_kernel(x_ref, w_ref, r_ref, o_ref):
    acc = jnp.dot(x_ref[...], w_ref[...], preferred_element_type=f32)
    o_ref[...] = (r_ref[...] + acc).astype(o_ref.dtype)


def matmul(x, w, *, col0=0, n=None, out_dtype=f32, residual=None, tm=1024, tn=1024):
    m, k = x.shape
    n = w.shape[1] - col0 if n is None else n
    tm, tn = min(tm, m), min(tn, n)
    assert col0 % tn == 0 and n % tn == 0 and m % tm == 0
    jb = col0 // tn
    est = 2 * (_nbytes((tm, k), bf16) + _nbytes((k, tn), bf16) + _nbytes((tm, tn), out_dtype)
               + _nbytes((tm, tn), f32)) + _nbytes((tm, tn), f32)
    in_specs = [pl.BlockSpec((tm, k), lambda i, j: (i, 0)), pl.BlockSpec((k, tn), lambda i, j: (0, jb + j))]
    args = [x, w]
    body = _mm_kernel
    if residual is not None:
        in_specs.append(pl.BlockSpec((tm, tn), lambda i, j: (i, j)))
        args.append(residual)
        body = _mm_res_kernel
    return pl.pallas_call(
        body,
        out_shape=jax.ShapeDtypeStruct((m, n), out_dtype),
        grid=(m // tm, n // tn),
        in_specs=in_specs,
        out_specs=pl.BlockSpec((tm, tn), lambda i, j: (i, j)),
        compiler_params=_params(("parallel", "parallel"), est),
        name="matmul",
    )(*args)


def _merge_kernel(hc_ref, at_ref, wc_ref, wa_ref, ga_ref, gb_ref, ba_ref, bb_ref, o_ref):
    conv_out = jnp.dot(hc_ref[...], wc_ref[...], preferred_element_type=f32)
    attn_out = jnp.dot(at_ref[...], wa_ref[...], preferred_element_type=f32)
    gate_a = jax.nn.sigmoid(ga_ref[...] + ba_ref[...])
    gate_b = jax.nn.sigmoid(gb_ref[...] + bb_ref[...])
    o_ref[...] = (gate_a * conv_out + gate_b * attn_out).astype(o_ref.dtype)


def gated_merge(hc, attn, w_conv_out, w_attn_out, proj, b_gate, gate_col0, *, tm=512, tn=512):
    m, k = hc.shape
    n = w_conv_out.shape[1]
    tm, tn = min(tm, m), min(tn, n)
    ga0, gb0 = gate_col0 // tn, (gate_col0 + n) // tn
    nb = n // tn
    est = 2 * (2 * _nbytes((tm, k), bf16) + 2 * _nbytes((k, tn), bf16) + 3 * _nbytes((tm, tn), f32)) \
        + 4 * _nbytes((tm, tn), f32)
    return pl.pallas_call(
        _merge_kernel,
        out_shape=jax.ShapeDtypeStruct((m, n), bf16),
        grid=(m // tm, nb),
        in_specs=[
            pl.BlockSpec((tm, k), lambda i, j: (i, 0)),
            pl.BlockSpec((tm, k), lambda i, j: (i, 0)),
            pl.BlockSpec((k, tn), lambda i, j: (0, j)),
            pl.BlockSpec((k, tn), lambda i, j: (0, j)),
            pl.BlockSpec((tm, tn), lambda i, j: (i, ga0 + j)),
            pl.BlockSpec((tm, tn), lambda i, j: (i, gb0 + j)),
            pl.BlockSpec((1, tn), lambda i, j: (0, j)),
            pl.BlockSpec((1, tn), lambda i, j: (0, nb + j)),
        ],
        out_specs=pl.BlockSpec((tm, tn), lambda i, j: (i, j)),
        compiler_params=_params(("parallel", "parallel"), est),
        name="gated_merge",
    )(hc, attn, w_conv_out, w_attn_out, proj, proj, b_gate, b_gate)


CONV_ROW_CHUNK = 64
CONV_LANE_CHUNK = 256
CONV_HALO = 32


def _conv_kernel(a_ref, gt_ref, pre_ref, wdw_ref, bdw_ref, lng_ref, lnb_ref, hc_ref, st_ref, ubuf, ybuf,
                 *, tt, width):
    t = pl.program_id(1)
    c = ubuf.shape[1]
    lead = CONV_HALO - (width - 1)
    rc = min(CONV_ROW_CHUNK, tt)

    @pl.when(t == 0)
    def _():
        ubuf[0:lead, :] = jnp.zeros((lead, c), f32)
        ubuf[lead:CONV_HALO, :] = pre_ref[...]

    ubuf[CONV_HALO:CONV_HALO + tt, :] = a_ref[...] * jax.nn.sigmoid(gt_ref[...])

    n_lane = c // CONV_LANE_CHUNK

    def chunk(idx, carry):
        r0 = pl.multiple_of((idx // n_lane) * rc, rc)
        c0 = pl.multiple_of((idx % n_lane) * CONV_LANE_CHUNK, CONV_LANE_CHUNK)
        win = ubuf[pl.ds(r0, rc + CONV_HALO), pl.ds(c0, CONV_LANE_CHUNK)]
        acc = jnp.broadcast_to(bdw_ref[:, pl.ds(c0, CONV_LANE_CHUNK)], (rc, CONV_LANE_CHUNK))
        for phase in range(V7X_SUBLANES):
            offs = [o for o in range(lead, lead + width) if o % V7X_SUBLANES == phase]
            if not offs:
                continue
            shifted = win if phase == 0 else pltpu.roll(win, rc + CONV_HALO - phase, 0)
            for o in offs:
                q = o - phase
                w_row = wdw_ref[o - lead:o - lead + 1, pl.ds(c0, CONV_LANE_CHUNK)]
                acc = acc + shifted[q:q + rc] * w_row
        ybuf[pl.ds(r0, rc), pl.ds(c0, CONV_LANE_CHUNK)] = acc
        return carry

    lax.fori_loop(0, (tt // rc) * n_lane, chunk, 0)

    def ln_chunk(i, carry):
        r0 = pl.multiple_of(i * rc, rc)
        y = ybuf[pl.ds(r0, rc), :]
        mu = jnp.mean(y, axis=-1, keepdims=True)
        yc = y - mu
        var = jnp.mean(yc * yc, axis=-1, keepdims=True)
        yn = yc * lax.rsqrt(var + EPS) * lng_ref[...] + lnb_ref[...]
        hc_ref[pl.ds(r0, rc), :] = (yn * jax.nn.sigmoid(yn)).astype(hc_ref.dtype)
        return carry

    lax.fori_loop(0, tt // rc, ln_chunk, 0)

    @pl.when(t == pl.num_programs(1) - 1)
    def _():
        st_ref[...] = ubuf[tt + lead:tt + CONV_HALO, :]

    ubuf[0:CONV_HALO, :] = ubuf[tt:tt + CONV_HALO, :]


def conformer_conv(proj, prefix, w_dw, b_dw, ln_g, ln_b, *, n_seq, seq_len, tt, out_dtype):
    width, c = w_dw.shape
    nt = seq_len // tt
    est = 2 * (2 * _nbytes((tt, c), f32) + _nbytes((width - 1, c), f32) + _nbytes((width, c), f32)
               + _nbytes((tt, c), out_dtype) + _nbytes((width - 1, c), f32)) \
        + _nbytes((2 * tt + CONV_HALO, c), f32) + 8 * _nbytes((CONV_ROW_CHUNK, c), f32)
    kern = functools.partial(_conv_kernel, tt=tt, width=width)
    return pl.pallas_call(
        kern,
        out_shape=(jax.ShapeDtypeStruct((n_seq * seq_len, c), out_dtype),
                   jax.ShapeDtypeStruct((n_seq, width - 1, c), f32)),
        grid=(n_seq, nt),
        in_specs=[
            pl.BlockSpec((tt, c), lambda b, t: (b * nt + t, 0)),
            pl.BlockSpec((tt, c), lambda b, t: (b * nt + t, 1)),
            pl.BlockSpec((None, width - 1, c), lambda b, t: (b, 0, 0)),
            pl.BlockSpec((width, c), lambda b, t: (0, 0)),
            pl.BlockSpec((1, c), lambda b, t: (0, 0)),
            pl.BlockSpec((1, c), lambda b, t: (0, 0)),
            pl.BlockSpec((1, c), lambda b, t: (0, 0)),
        ],
        out_specs=(pl.BlockSpec((tt, c), lambda b, t: (b * nt + t, 0)),
                   pl.BlockSpec((None, width - 1, c), lambda b, t: (b, 0, 0))),
        scratch_shapes=[pltpu.VMEM((tt + CONV_HALO, c), f32), pltpu.VMEM((tt, c), f32)],
        compiler_params=_params(("parallel", "arbitrary"), est),
        name="conformer_conv",
    )(proj, proj, prefix, w_dw, b_dw.reshape(1, c), ln_g.reshape(1, c), ln_b.reshape(1, c))


def _rel_bucket(dist):
    n = jnp.maximum(dist, 0)
    max_exact = NUM_BUCKETS // 2
    large = max_exact + (jnp.log(jnp.maximum(n, 1).astype(f32) / max_exact)
                         / math.log(MAX_DISTANCE / max_exact) * (NUM_BUCKETS - max_exact)).astype(jnp.int32)
    large = jnp.minimum(large, NUM_BUCKETS - 1)
    return jnp.where(n < max_exact, n, large)


def _bias_from_dist(dist, lookup):
    bucket = _rel_bucket(dist)
    out = jnp.zeros(dist.shape, f32)
    for b in range(NUM_BUCKETS):
        out = jnp.where(bucket == b, lookup(b), out)
    return jnp.where(dist >= 0, out, NEG)


def _prompt_bias_kernel(rb_ref, o_ref):
    h = pl.program_id(0)
    ki = lax.broadcasted_iota(jnp.int32, (MOBA_BLOCK, MOBA_BLOCK), 0)
    qi = lax.broadcasted_iota(jnp.int32, (MOBA_BLOCK, MOBA_BLOCK), 1)
    for tab in range(3):
        dist = tab * MOBA_BLOCK + qi - ki
        o_ref[tab] = _bias_from_dist(dist, lambda b: rb_ref[b, h])


def prompt_bias_tables(rel_bias):
    n_heads = rel_bias.shape[1]
    return pl.pallas_call(
        _prompt_bias_kernel,
        out_shape=jax.ShapeDtypeStruct((n_heads, 3, MOBA_BLOCK, MOBA_BLOCK), f32),
        grid=(n_heads,),
        in_specs=[pl.BlockSpec(memory_space=pltpu.SMEM)],
        out_specs=pl.BlockSpec((None, 3, MOBA_BLOCK, MOBA_BLOCK), lambda h: (h, 0, 0, 0)),
        compiler_params=_params(("parallel",), 16 * 1024 * 1024),
        name="prompt_bias_tables",
    )(rel_bias)


def _sample_bias_kernel(rb_ref, page_ref, own_ref, *, past_len, page, n_q, n_near, n_heads):
    n_rows = n_heads * n_q
    rowh = lax.broadcasted_iota(jnp.int32, (n_rows, 1), 0) // n_q

    def lookup(b):
        out = jnp.zeros((n_rows, 1), f32)
        for hh in range(n_heads):
            out = jnp.where(rowh == hh, rb_ref[b, hh], out)
        return out

    def table(width, col_head, dist):
        row = lax.broadcasted_iota(jnp.int32, (n_rows, width), 0)
        col = lax.broadcasted_iota(jnp.int32, (n_rows, width), 1)
        same_head = (row // n_q) == col_head(col)
        return jnp.where(same_head, _bias_from_dist(dist(row % n_q, col), lookup), NEG)

    n_pages = past_len // page
    wide = page * n_heads
    page_ref[0] = table(wide, lambda c: c % n_heads, lambda q, c: jnp.full(c.shape, MAX_DISTANCE, jnp.int32))
    for i in range(n_near):
        pg = n_pages - n_near + i
        page_ref[1 + i] = table(wide, lambda c: c % n_heads,
                                lambda q, c: past_len + q - (pg * page + c // n_heads))
    own_ref[...] = table(n_rows, lambda c: c // n_q, lambda q, c: q - c % n_q)


def sample_bias_tables(rel_bias, *, past_len, page, n_q, n_near):
    n_heads = rel_bias.shape[1]
    n_rows = n_heads * n_q
    kern = functools.partial(_sample_bias_kernel, past_len=past_len, page=page, n_q=n_q, n_near=n_near,
                             n_heads=n_heads)
    return pl.pallas_call(
        kern,
        out_shape=(jax.ShapeDtypeStruct((n_near + 1, n_rows, page * n_heads), f32),
                   jax.ShapeDtypeStruct((n_rows, n_rows), f32)),
        in_specs=[pl.BlockSpec(memory_space=pltpu.SMEM)],
        out_specs=(pl.BlockSpec(memory_space=pltpu.VMEM), pl.BlockSpec(memory_space=pltpu.VMEM)),
        name="sample_bias_tables",
    )(rel_bias)


MOBA_HEAD_GROUP = 4


def _moba_prompt_kernel(q_ref, k_ref, v_ref, bias_ref, o_ref,
                        kb_ref, vt_ref, mean_ref, qb_ref, sel_ref, m_ref, l_ref, acc_ref,
                        s_ref, p_ref, alpha_ref, *, head_dim):
    c = pl.program_id(2)
    n_blocks = k_ref.shape[0] // MOBA_BLOCK
    scale = head_dim ** -0.5

    for hh in range(MOBA_HEAD_GROUP):
        lanes = slice(hh * head_dim, (hh + 1) * head_dim)

        @pl.when(c == 0)
        def _():
            kf = k_ref[:, lanes]
            kb_ref[hh] = kf.astype(bf16)
            vt_ref[hh] = v_ref[:, lanes].T.astype(bf16)
            mean_ref[hh] = jnp.mean(kf.reshape(n_blocks, MOBA_BLOCK, head_dim), axis=1)

        qf = q_ref[:, lanes]
        st = lax.dot_general(mean_ref[hh], qf, NT_DIMS, preferred_element_type=f32,
                             precision=lax.Precision.HIGHEST)
        row = lax.broadcasted_iota(jnp.int32, st.shape, 0)
        sm = jnp.where(row < c, st, NEG)
        for j in range(n_blocks):
            rj = sm[j:j + 1, :]
            beats = jnp.where((sm > rj) | ((sm == rj) & (row < j)), 1.0, 0.0)
            rank = jnp.sum(beats, axis=0, keepdims=True)
            jv = jnp.full(rank.shape, j, jnp.int32)
            chosen = jnp.where((rank < MOBA_TOPK) & (jv < c), 1.0, 0.0)
            sel_ref[hh, j] = jnp.where(jv == c, 1.0, chosen)

        qb_ref[hh] = qf.astype(bf16)
        m_ref[hh] = jnp.full(m_ref.shape[1:], NEG, f32)
        l_ref[hh] = jnp.zeros(l_ref.shape[1:], f32)
        acc_ref[hh] = jnp.zeros(acc_ref.shape[1:], f32)

    def tile(i, carry):
        j = c - i
        tab = jnp.minimum(i, 2)
        k0 = pl.multiple_of(j * MOBA_BLOCK, MOBA_BLOCK)
        for hh in range(MOBA_HEAD_GROUP):
            kj = kb_ref[hh, pl.ds(k0, MOBA_BLOCK), :]
            s_ref[hh] = lax.dot_general(kj, qb_ref[hh], NT_DIMS, preferred_element_type=f32)
        for hh in range(MOBA_HEAD_GROUP):
            lg = s_ref[hh] * scale + bias_ref[hh, tab]
            lg = jnp.where(sel_ref[hh, j] > 0.5, lg, NEG)
            m_old = m_ref[hh]
            m_new = jnp.maximum(m_old, jnp.max(lg, axis=0, keepdims=True))
            p = jnp.exp(lg - m_new)
            alpha = jnp.exp(m_old - m_new)
            l_ref[hh] = alpha * l_ref[hh] + jnp.sum(p, axis=0, keepdims=True)
            p_ref[hh] = p.astype(bf16)
            alpha_ref[hh] = alpha
            m_ref[hh] = m_new
        for hh in range(MOBA_HEAD_GROUP):
            vtj = vt_ref[hh, :, pl.ds(k0, MOBA_BLOCK)]
            acc_ref[hh] = alpha_ref[hh] * acc_ref[hh] + jnp.dot(vtj, p_ref[hh], preferred_element_type=f32)
        return carry

    lax.fori_loop(0, c + 1, tile, 0)
    for hh in range(MOBA_HEAD_GROUP):
        o_ref[:, hh * head_dim:(hh + 1) * head_dim] = (acc_ref[hh] / l_ref[hh]).T.astype(o_ref.dtype)


def moba_prompt(proj, k, v, bias_tabs, *, n_seq, seq_len, n_heads, head_dim, q_col0):
    width = n_heads * head_dim
    gw = MOBA_HEAD_GROUP * head_dim
    n_groups = n_heads // MOBA_HEAD_GROUP
    n_blocks = seq_len // MOBA_BLOCK
    qb0 = q_col0 // gw
    est = 2 * (2 * _nbytes((MOBA_BLOCK, gw), f32) + 2 * _nbytes((seq_len, gw), f32)
               + _nbytes((MOBA_HEAD_GROUP, 3, MOBA_BLOCK, MOBA_BLOCK), f32)) \
        + 2 * _nbytes((MOBA_HEAD_GROUP, seq_len, head_dim), bf16) + 8 * _nbytes((MOBA_BLOCK, MOBA_BLOCK), f32)
    kern = functools.partial(_moba_prompt_kernel, head_dim=head_dim)
    return pl.pallas_call(
        kern,
        out_shape=jax.ShapeDtypeStruct((n_seq * seq_len, width), bf16),
        grid=(n_seq, n_groups, n_blocks),
        in_specs=[
            pl.BlockSpec((MOBA_BLOCK, gw), lambda b, g, c: (b * n_blocks + c, qb0 + g)),
            pl.BlockSpec((seq_len, gw), lambda b, g, c: (b, g)),
            pl.BlockSpec((seq_len, gw), lambda b, g, c: (b, g)),
            pl.BlockSpec((MOBA_HEAD_GROUP, 3, MOBA_BLOCK, MOBA_BLOCK), lambda b, g, c: (g, 0, 0, 0)),
        ],
        out_specs=pl.BlockSpec((MOBA_BLOCK, gw), lambda b, g, c: (b * n_blocks + c, g)),
        scratch_shapes=[
            pltpu.VMEM((MOBA_HEAD_GROUP, seq_len, head_dim), bf16),
            pltpu.VMEM((MOBA_HEAD_GROUP, head_dim, seq_len), bf16),
            pltpu.VMEM((MOBA_HEAD_GROUP, n_blocks, head_dim), f32),
            pltpu.VMEM((MOBA_HEAD_GROUP, MOBA_BLOCK, head_dim), bf16),
            pltpu.VMEM((MOBA_HEAD_GROUP, n_blocks, 1, MOBA_BLOCK), f32),
            pltpu.VMEM((MOBA_HEAD_GROUP, 1, MOBA_BLOCK), f32),
            pltpu.VMEM((MOBA_HEAD_GROUP, 1, MOBA_BLOCK), f32),
            pltpu.VMEM((MOBA_HEAD_GROUP, head_dim, MOBA_BLOCK), f32),
            pltpu.VMEM((MOBA_HEAD_GROUP, MOBA_BLOCK, MOBA_BLOCK), f32),
            pltpu.VMEM((MOBA_HEAD_GROUP, MOBA_BLOCK, MOBA_BLOCK), bf16),
            pltpu.VMEM((MOBA_HEAD_GROUP, 1, MOBA_BLOCK), f32),
        ],
        compiler_params=_params(("parallel", "parallel", "arbitrary"), est),
        name="moba_prompt",
    )(proj, k, v, bias_tabs)


def _moba_sample_kernel(pt_ref, q_ref, kn_ref, vn_ref, *rest,
                        n_heads, head_dim, n_q, n_near, n_pages, pages_per_block, blocks_per_step):
    del pt_ref
    pages_per_step = pages_per_block * blocks_per_step
    kc_refs = rest[:pages_per_step]
    vc_refs = rest[pages_per_step:2 * pages_per_step]
    tab_ref, own_ref, o_ref, qb_ref, qf_ref, own_kv_ref, score_ref, m_ref, l_ref, part_ref = rest[2 * pages_per_step:]
    s = pl.program_id(1)
    n_blocks = n_pages // pages_per_block
    page = kc_refs[0].shape[0]
    n_rows = n_heads * n_q
    scale = head_dim ** -0.5
    lane = lax.broadcasted_iota(jnp.int32, (n_rows, V7X_LANES), 1)

    @pl.when(s == 0)
    def _():
        for h in range(n_heads):
            qf_ref[h * n_q:(h + 1) * n_q, :] = q_ref[:, h * head_dim:(h + 1) * head_dim]
        qb_ref[...] = qf_ref[...].astype(bf16)
        score_ref[...] = jnp.full(score_ref.shape, -jnp.inf, f32)
        m_ref[...] = jnp.full(m_ref.shape, NEG, f32)
        l_ref[...] = jnp.zeros(l_ref.shape, f32)

    def record(ref, slot, col):
        ref[...] = jnp.where(lane == slot, col, ref[...])

    def partial_softmax(slots, logits, values):
        ps = []
        for slot, lg in zip(slots, logits):
            m = jnp.max(lg, axis=1, keepdims=True)
            p = jnp.exp(lg - m)
            record(m_ref, slot, m)
            record(l_ref, slot, jnp.sum(p, axis=1, keepdims=True))
            ps.append(p.astype(bf16))
        for slot, p, v in zip(slots, ps, values):
            part_ref[slot] = jnp.dot(p, v, preferred_element_type=f32)

    def past_block(blk, k_refs, v_refs):
        slots, logits, values = [], [], []
        ksum = jnp.zeros((n_heads, head_dim), f32)
        for i in range(pages_per_block):
            pg = blk * pages_per_block + i
            k3 = k_refs[i][...]
            ksum = ksum + jnp.sum(k3, axis=0)
            k2 = k3.reshape(page * n_heads, head_dim).astype(bf16)
            raw = lax.dot_general(qb_ref[...], k2, NT_DIMS, preferred_element_type=f32)
            near = pg - (n_pages - n_near)
            slots.append(pg)
            logits.append(raw * scale + tab_ref[jnp.maximum(near + 1, 0)])
            values.append(v_refs[i][...].reshape(page * n_heads, head_dim).astype(bf16))
        ksum_rows = jnp.concatenate(
            [jnp.broadcast_to(ksum[h:h + 1, :], (n_q, head_dim)) for h in range(n_heads)], axis=0)
        record(score_ref, blk, jnp.sum(qf_ref[...] * ksum_rows, axis=1, keepdims=True))
        partial_softmax(slots, logits, values)

    @pl.when(s < n_blocks // blocks_per_step)
    def _():
        for j in range(blocks_per_step):
            pages = slice(j * pages_per_block, (j + 1) * pages_per_block)
            past_block(s * blocks_per_step + j, kc_refs[pages], vc_refs[pages])

    @pl.when(s == n_blocks // blocks_per_step)
    def _():
        for h in range(n_heads):
            own_kv_ref[0, h * n_q:(h + 1) * n_q, :] = kn_ref[:, h * head_dim:(h + 1) * head_dim]
            own_kv_ref[1, h * n_q:(h + 1) * n_q, :] = vn_ref[:, h * head_dim:(h + 1) * head_dim]
        raw = lax.dot_general(qb_ref[...], own_kv_ref[0].astype(bf16), NT_DIMS, preferred_element_type=f32)
        partial_softmax([n_pages], [raw * scale + own_ref[...]], [own_kv_ref[1].astype(bf16)])

        score = score_ref[...]
        sel = jnp.where(lane == n_pages, 1.0, 0.0)
        for _ in range(MOBA_TOPK):
            mx = jnp.max(score, axis=1, keepdims=True)
            first = jnp.min(jnp.where(score == mx, lane, V7X_LANES), axis=1, keepdims=True)
            sel = jnp.where((lane // pages_per_block == first) & (lane < n_pages), 1.0, sel)
            score = jnp.where(lane == first, -jnp.inf, score)

        m_all, l_all = m_ref[...], l_ref[...]
        m_top = jnp.max(jnp.where(sel > 0.5, m_all, NEG), axis=1, keepdims=True)
        w = jnp.where(sel > 0.5, jnp.exp(m_all - m_top), 0.0)
        wn = w / jnp.sum(w * l_all, axis=1, keepdims=True)
        out = jnp.zeros((n_rows, head_dim), f32)
        for i in range(n_pages + 1):
            out = out + wn[:, i:i + 1] * part_ref[i]
        for h in range(n_heads):
            o_ref[:, h * head_dim:(h + 1) * head_dim] = out[h * n_q:(h + 1) * n_q, :].astype(o_ref.dtype)


def moba_sample(proj, k_new, v_new, cache_k, cache_v, page_table, page_tabs, own_tab, *,
                layer, n_seq, n_q, q_col0, n_near):
    _, _, page, n_heads, head_dim = cache_k.shape
    width = n_heads * head_dim
    n_rows = n_heads * n_q
    n_pages = page_table.shape[1]
    pages_per_block = MOBA_BLOCK // page
    n_blocks = n_pages // pages_per_block
    qb0 = q_col0 // width
    blocks_per_step = 2 if n_blocks % 2 == 0 else 1
    pages_per_step = pages_per_block * blocks_per_step
    assert n_rows == V7X_LANES and MOBA_TOPK <= n_blocks and n_pages < V7X_LANES and n_pages % pages_per_block == 0
    page_bytes = _nbytes((page, n_heads, head_dim), f32)
    est = 2 * (4 * _nbytes((n_q, width), f32) + 2 * pages_per_step * page_bytes
               + _nbytes(page_tabs.shape, f32) + _nbytes(own_tab.shape, f32)) \
        + _nbytes((n_pages + 1, n_rows, head_dim), f32) + 8 * pages_per_block * page_bytes
    kern = functools.partial(_moba_sample_kernel, n_heads=n_heads, head_dim=head_dim, n_q=n_q, n_near=n_near,
                             n_pages=n_pages, pages_per_block=pages_per_block, blocks_per_step=blocks_per_step)

    def page_spec(i):
        def index(n, s, pt):
            return (layer, pt[n, jnp.minimum(s * pages_per_step + i, n_pages - 1)], 0, 0, 0)
        return pl.BlockSpec((None, None, page, n_heads, head_dim), index)

    pages = [page_spec(i) for i in range(pages_per_step)]
    stat = pltpu.VMEM((n_rows, V7X_LANES), f32)
    grid_spec = pltpu.PrefetchScalarGridSpec(
        num_scalar_prefetch=1,
        grid=(n_seq, n_blocks // blocks_per_step + 1),
        in_specs=[
            pl.BlockSpec((n_q, width), lambda n, s, pt: (n, qb0)),
            pl.BlockSpec((n_q, width), lambda n, s, pt: (n, 0)),
            pl.BlockSpec((n_q, width), lambda n, s, pt: (n, 0)),
            *pages, *pages,
            pl.BlockSpec(page_tabs.shape, lambda n, s, pt: (0, 0, 0)),
            pl.BlockSpec(own_tab.shape, lambda n, s, pt: (0, 0)),
        ],
        out_specs=pl.BlockSpec((n_q, width), lambda n, s, pt: (n, 0)),
        scratch_shapes=[
            pltpu.VMEM((n_rows, head_dim), bf16),
            pltpu.VMEM((n_rows, head_dim), f32),
            pltpu.VMEM((2, n_rows, head_dim), f32),
            stat, stat, stat,
            pltpu.VMEM((n_pages + 1, n_rows, head_dim), f32),
        ],
    )
    return pl.pallas_call(
        kern,
        out_shape=jax.ShapeDtypeStruct((n_seq * n_q, width), f32),
        grid_spec=grid_spec,
        compiler_params=_params(("parallel", "arbitrary"), est),
        name="moba_sample",
    )(page_table, proj, k_new, v_new, *([cache_k] * pages_per_step), *([cache_v] * pages_per_step),
      page_tabs, own_tab)


def _peer_score_kernel(q_ref, sk_ref, ns1_ref, t2_ref, e1_ref, e2_ref, *, n_heads, n_keys):
    dk = sk_ref.shape[3]
    neg_inf = -jnp.inf
    n_top = PEER_TOPK + 1
    pairs = [(i, j) for i in range(n_top) for j in range(n_top) if (i + 1) * (j + 1) <= n_top]
    n_cand = -(-len(pairs) // V7X_SUBLANES) * V7X_SUBLANES
    for h in range(n_heads):
        scores, tops = [], []
        for part in range(2):
            c0 = (2 * h + part) * dk
            s = lax.dot_general(sk_ref[h, part], q_ref[:, c0:c0 + dk], NT_DIMS, preferred_element_type=f32)
            scores.append(s)
            vals, w = [], s
            for _ in range(n_top):
                mx = jnp.max(w, axis=0, keepdims=True)
                vals.append(mx)
                w = jnp.where(w == mx, neg_inf, w)
            tops.append(vals)
        a, b = tops
        tm = a[0].shape[1]
        cand = jnp.concatenate([a[i] + b[j] for i, j in pairs]
                               + [jnp.full((n_cand - len(pairs), tm), neg_inf, f32)], axis=0)
        rowi = lax.broadcasted_iota(jnp.int32, cand.shape, 0)
        best = []
        for _ in range(n_top):
            mx = jnp.max(cand, axis=0, keepdims=True)
            first = jnp.min(jnp.where(cand == mx, rowi, n_cand), axis=0, keepdims=True)
            best.append(mx)
            cand = jnp.where(rowi == first, neg_inf, cand)
        z = jnp.zeros_like(best[0])
        for c in best[:PEER_TOPK]:
            z = z + jnp.exp(c - best[0])
        theta = 0.5 * (best[PEER_TOPK - 1] + best[PEER_TOPK])
        ns1 = -scores[0]
        t2 = scores[1] - theta
        e1 = jnp.exp(scores[0] - a[0])
        e2 = jnp.exp(scores[1] - b[0]) / z
        for c in range(tm // V7X_LANES):
            cols = slice(c * V7X_LANES, (c + 1) * V7X_LANES)
            ns1_ref[h, c] = ns1[:, cols]
            t2_ref[h, c] = t2[:, cols]
            e1_ref[h, c] = e1[:, cols]
            e2_ref[h, c] = e2[:, cols]


def peer_scores(qp, subkeys, *, tm):
    t = qp.shape[0]
    n_heads, _, n_keys, dk = subkeys.shape
    tm = min(tm, t)
    big = jax.ShapeDtypeStruct((n_heads, t // V7X_LANES, n_keys, V7X_LANES), f32)
    blk = pl.BlockSpec((n_heads, tm // V7X_LANES, n_keys, V7X_LANES), lambda i: (0, i, 0, 0))
    est = 2 * (_nbytes((tm, qp.shape[1]), bf16) + _nbytes(subkeys.shape, bf16) + 4 * _nbytes((n_heads, n_keys, tm), f32)) \
        + 16 * _nbytes((n_keys, tm), f32)
    kern = functools.partial(_peer_score_kernel, n_heads=n_heads, n_keys=n_keys)
    return pl.pallas_call(
        kern,
        out_shape=(big, big, big, big),
        grid=(t // tm,),
        in_specs=[pl.BlockSpec((tm, qp.shape[1]), lambda i: (i, 0)),
                  pl.BlockSpec(subkeys.shape, lambda i: (0, 0, 0, 0))],
        out_specs=(blk, blk, blk, blk),
        compiler_params=_params(("parallel",), est),
        name="peer_scores",
    )(qp, subkeys)


def _peer_dense_kernel(x_ref, u_ref, v_ref, ns1_ref, t2_ref, e1_ref, e2_ref, h_ref, g_ref, o_ref,
                       acc_ref, act_ref, wact_ref, *, n_heads, n_keys):
    e = pl.program_id(1)
    te, tm = act_ref.shape
    rows_per_step = te // n_keys

    @pl.when(e == 0)
    def _():
        acc_ref[...] = jnp.zeros(acc_ref.shape, f32)

    act_ref[...] = lax.dot_general(u_ref[...], x_ref[...], NT_DIMS, preferred_element_type=f32)

    for il in range(rows_per_step):
        i1 = e * rows_per_step + il
        rows = slice(il * n_keys, (il + 1) * n_keys)
        for c in range(tm // V7X_LANES):
            cols = slice(c * V7X_LANES, (c + 1) * V7X_LANES)
            wt = jnp.zeros((n_keys, V7X_LANES), f32)
            for h in range(n_heads):
                chosen = t2_ref[h, c] >= ns1_ref[h, c, pl.ds(i1, 1), :]
                weight = e2_ref[h, c] * e1_ref[h, c, pl.ds(i1, 1), :]
                wt = wt + jnp.where(chosen, weight, 0.0)
            wact_ref[cols, rows] = (wt * jax.nn.gelu(act_ref[rows, cols])).T.astype(bf16)

    acc_ref[...] += jnp.dot(wact_ref[...], v_ref[...], preferred_element_type=f32)

    @pl.when(e == pl.num_programs(1) - 1)
    def _():
        r = h_ref[...] + acc_ref[...]
        ms = jnp.mean(r * r, axis=-1, keepdims=True)
        o_ref[...] = r * lax.rsqrt(ms + EPS) * g_ref[...]


def peer_dense(xn, u_tab, v_tab, ns1, t2, e1, e2, h, g, *, tm, te):
    t, d = xn.shape
    n_exp = u_tab.shape[0]
    n_heads, _, n_keys, _ = ns1.shape
    tm = min(tm, t)
    once = dict(pipeline_mode=pl.Buffered(1))
    sblk = pl.BlockSpec((n_heads, tm // V7X_LANES, n_keys, V7X_LANES), lambda i, e: (0, i, 0, 0), **once)
    est = _nbytes((tm, d), bf16) + 4 * _nbytes((te, d), bf16) + 4 * _nbytes((n_heads, n_keys, tm), f32) \
        + 4 * _nbytes((tm, d), f32) + 2 * _nbytes((te, tm), f32) + 8 * _nbytes((te, tm), f32)
    kern = functools.partial(_peer_dense_kernel, n_heads=n_heads, n_keys=n_keys)
    return pl.pallas_call(
        kern,
        out_shape=jax.ShapeDtypeStruct((t, d), f32),
        grid=(t // tm, n_exp // te),
        in_specs=[
            pl.BlockSpec((tm, d), lambda i, e: (i, 0), **once),
            pl.BlockSpec((te, d), lambda i, e: (e, 0)),
            pl.BlockSpec((te, d), lambda i, e: (e, 0)),
            sblk, sblk, sblk, sblk,
            pl.BlockSpec((tm, d), lambda i, e: (i, 0), **once),
            pl.BlockSpec((1, d), lambda i, e: (0, 0)),
        ],
        out_specs=pl.BlockSpec((tm, d), lambda i, e: (i, 0)),
        scratch_shapes=[pltpu.VMEM((tm, d), f32), pltpu.VMEM((te, tm), f32), pltpu.VMEM((tm, te), bf16)],
        compiler_params=_params(("parallel", "arbitrary"), est),
        name="peer_dense",
    )(xn, u_tab, v_tab, ns1, t2, e1, e2, h, g.reshape(1, d))


def _layer(x, prefix, attend, conv_tt, conv_dtype, lw, norm_final_g, *, n_seq, seq_len, peer_tm, peer_te):
    (norm_mix_g, w_in, b_gate, conv_w_dw, conv_b_dw, conv_ln_g, conv_ln_b, w_conv_out, w_attn_out, w_out,
     norm_ffn_g, peer_wq, peer_subkeys, peer_u, peer_v) = lw
    d = x.shape[1]
    conv_ch = conv_w_dw.shape[1]
    attn_w = w_attn_out.shape[0]
    q_col0 = 2 * conv_ch
    gate_col0 = q_col0 + 3 * attn_w

    xn = rmsnorm(x, norm_mix_g)
    proj = matmul(xn, w_in, col0=0, n=q_col0 + attn_w)
    k = matmul(xn, w_in, col0=q_col0 + attn_w, n=attn_w)
    v = matmul(xn, w_in, col0=q_col0 + 2 * attn_w, n=attn_w)
    gates = matmul(xn, w_in, col0=gate_col0)
    hc, conv_state = conformer_conv(proj, prefix, conv_w_dw, conv_b_dw, conv_ln_g, conv_ln_b,
                                    n_seq=n_seq, seq_len=seq_len, tt=conv_tt, out_dtype=conv_dtype)
    attn = attend(proj, k, v, q_col0)
    merged = gated_merge(hc.astype(bf16), attn.astype(bf16), w_conv_out, w_attn_out, gates,
                         b_gate.reshape(1, -1), 0)
    h = matmul(merged, w_out, residual=x)
    xn2 = rmsnorm(h, norm_ffn_g)
    qp = matmul(xn2, peer_wq, out_dtype=bf16)
    ns1, t2, e1, e2 = peer_scores(qp, peer_subkeys, tm=peer_tm)
    y = peer_dense(xn2, peer_u, peer_v, ns1, t2, e1, e2, h, norm_final_g, tm=peer_tm, te=peer_te)
    return y, k, v, conv_state


def kernel(x_prompt, x_sample, cache_k, cache_v, state_conv, page_table, rel_bias, norm_mix_g, w_in, b_gate,
           conv_w_dw, conv_b_dw, conv_ln_g, conv_ln_b, w_conv_out, w_attn_out, w_out, norm_ffn_g,
           peer_wq, peer_subkeys, peer_u, peer_v, norm_final_g):
    depth = w_in.shape[0]
    assert depth == 1, "the final norm is fused into the last layer's PEER kernel"
    n_p, seq_p, d = x_prompt.shape
    n_s, seq_s, _ = x_sample.shape
    _, n_pool, page, n_heads, head_dim = cache_k.shape
    width = n_heads * head_dim
    past_len = page_table.shape[1] * page
    conv_w = conv_w_dw.shape[1]
    conv_ch = conv_w_dw.shape[2]

    n_near = sum(1 for pg in range(past_len // page) if past_len - (pg * page + page - 1) < MAX_DISTANCE)

    l = 0
    lw = (norm_mix_g[l], w_in[l].astype(bf16), b_gate[l], conv_w_dw[l], conv_b_dw[l], conv_ln_g[l], conv_ln_b[l],
          w_conv_out[l].astype(bf16), w_attn_out[l].astype(bf16), w_out[l].astype(bf16), norm_ffn_g[l],
          peer_wq[l].astype(bf16), peer_subkeys[l].astype(bf16), peer_u[l].astype(bf16), peer_v[l].astype(bf16))

    p_tabs = prompt_bias_tables(rel_bias)
    page_tabs, own_tab = sample_bias_tables(rel_bias, past_len=past_len, page=page, n_q=seq_s, n_near=n_near)

    attend_p = lambda proj, k, v, q_col0: moba_prompt(proj, k, v, p_tabs, n_seq=n_p, seq_len=seq_p,
                                                      n_heads=n_heads, head_dim=head_dim, q_col0=q_col0)
    attend_s = lambda proj, k, v, q_col0: moba_sample(proj, k, v, cache_k, cache_v, page_table, page_tabs, own_tab,
                                                      layer=l, n_seq=n_s, n_q=seq_s, q_col0=q_col0, n_near=n_near)

    zeros_prefix = jnp.zeros((n_p, conv_w - 1, conv_ch), f32)
    yp, kp, vp, cp = _layer(x_prompt.reshape(n_p * seq_p, d), zeros_prefix, attend_p, 256, bf16, lw, norm_final_g,
                            n_seq=n_p, seq_len=seq_p, peer_tm=512, peer_te=1024)
    ys, ks, vs, cs = _layer(x_sample.reshape(n_s * seq_s, d), state_conv[l], attend_s, seq_s, f32, lw, norm_final_g,
                            n_seq=n_s, seq_len=seq_s, peer_tm=256, peer_te=512)

    return (yp.reshape(n_p, seq_p, d), ys.reshape(n_s, seq_s, d),
            kp.reshape(1, n_p, seq_p, n_heads, head_dim), vp.reshape(1, n_p, seq_p, n_heads, head_dim),
            ks.reshape(1, n_s, seq_s, n_heads, head_dim), vs.reshape(1, n_s, seq_s, n_heads, head_dim),
            cp[None], cs[None])
```
